```python
import math
import jax, jax.numpy as jnp
from jax import lax
import numpy as np

D_MODEL = 4096
BATCH = 4
SEQ = 4096
DEPTH = 1
DEC_BATCH = 8
DEC_SEQ = 2048
PAST_LEN = 128

N_META = 16
HEAD_DIM = 128
MIX_WIDTH = D_MODEL
ATTN_WIDTH = 3 * MIX_WIDTH // 4
N_HEADS = ATTN_WIDTH // (2 * HEAD_DIM)
ATTN_QK_WIDTH = N_HEADS * 2 * HEAD_DIM
ATTN_V_WIDTH = N_HEADS * 2 * HEAD_DIM
FOURIER_WIDTH = MIX_WIDTH // 4
N_FOURIER_GROUPS = 4
FOURIER_GROUP = FOURIER_WIDTH // N_FOURIER_GROUPS
N_BRANCHES = 2
IN_WIDTH = 2 * ATTN_QK_WIDTH + ATTN_V_WIDTH + FOURIER_WIDTH + N_BRANCHES * D_MODEL
ROPE_THETA = 10000.0
Q_BLOCK = 128
N_EXPERTS = 32
TOP_K = 4
D_EXPERT = D_MODEL
SWIGLU_LIMIT = 7.0
SWIGLU_ALPHA = 1.702
MOE_BLOCK = 128
EPS = 1e-6

kernel_name = "hybrid_fnet_diffattn_moe_encoder"


def _lambda_init(layer):
    return 0.8 - 0.6 * math.exp(-0.3 * layer)


def _rmsnorm(x, w):
    xf = x.astype(jnp.float32)
    y = xf * lax.rsqrt(jnp.mean(xf * xf, axis=-1, keepdims=True) + EPS)
    return (y * w.astype(jnp.float32)).astype(x.dtype)


def _rope_tables(length):
    inv_freq = 1.0 / (ROPE_THETA ** (jnp.arange(0, HEAD_DIM, 2, dtype=jnp.float32) / HEAD_DIM))
    ang = jnp.arange(length, dtype=jnp.float32)[:, None] * inv_freq[None, :]
    ang = jnp.concatenate([ang, ang], axis=-1)
    return jnp.cos(ang), jnp.sin(ang)


def _apply_rope(x, cos, sin):
    xf = x.astype(jnp.float32)
    half = HEAD_DIM // 2
    rot = jnp.concatenate([-xf[..., half:], xf[..., :half]], axis=-1)
    c = cos[None, :, None, None, :]
    s = sin[None, :, None, None, :]
    return (xf * c + rot * s).astype(x.dtype)


def _diff_attention(q, k, v, lam):
    B, L = q.shape[0], q.shape[1]
    S = L - N_META
    scale = HEAD_DIM ** -0.5

    def attend(qb):
        s = jnp.einsum('bqhmd,bkhmd->bmhqk', qb, k).astype(jnp.float32) * scale
        p = jax.nn.softmax(s, axis=-1)
        w = (p[:, 0] - lam * p[:, 1]).astype(v.dtype)
        return jnp.einsum('bhqk,bkhe->bqhe', w, v)

    out_meta = attend(q[:, :N_META])
    qr = q[:, N_META:].reshape(B, S // Q_BLOCK, Q_BLOCK, N_HEADS, 2, HEAD_DIM)
    qr = jnp.transpose(qr, (1, 0, 2, 3, 4, 5))
    out_real = lax.map(attend, qr)
    out_real = jnp.transpose(out_real, (1, 0, 2, 3, 4)).reshape(B, S, N_HEADS, 2 * HEAD_DIM)
    return jnp.concatenate([out_meta, out_real], axis=1)


def _fourier_mix(u):
    B, L = u.shape[0], u.shape[1]
    ug = u.reshape(B, L, N_FOURIER_GROUPS, FOURIER_GROUP).astype(jnp.float32)
    f = jnp.fft.fft2(ug, axes=(1, 3), norm='ortho').real
    return f.reshape(B, L, FOURIER_WIDTH).astype(u.dtype)


def _moe(xn, w_router, b_router, w_gate_up, b_gate_up, w_down, b_down):
    T, D = xn.shape
    logits = (xn @ w_router).astype(jnp.float32) + b_router.astype(jnp.float32)
    top_val, top_idx = lax.top_k(logits, TOP_K)
    gates = jax.nn.softmax(top_val, axis=-1)
    A = T * TOP_K
    e_flat = top_idx.reshape(-1).astype(jnp.int32)
    t_flat = jnp.repeat(jnp.arange(T, dtype=jnp.int32), TOP_K)
    g_flat = gates.reshape(-1)
    order = jnp.argsort(e_flat)
    e_sorted = e_flat[order]
    t_sorted = t_flat[order]
    g_sorted = g_flat[order]
    counts = jnp.bincount(e_flat, length=N_EXPERTS)
    starts = jnp.cumsum(counts) - counts
    padded = ((counts + MOE_BLOCK - 1) // MOE_BLOCK) * MOE_BLOCK
    pends = jnp.cumsum(padded)
    pstarts = pends - padded
    dest = pstarts[e_sorted] + (jnp.arange(A, dtype=jnp.int32) - starts[e_sorted])
    n_blocks = -(-A // MOE_BLOCK) + N_EXPERTS
    tok_buf = jnp.full((n_blocks * MOE_BLOCK,), T, jnp.int32).at[dest].set(t_sorted)
    gate_buf = jnp.zeros((n_blocks * MOE_BLOCK,), jnp.float32).at[dest].set(g_sorted)
    blk_exp = jnp.minimum(
        jnp.searchsorted(pends, jnp.arange(n_blocks, dtype=jnp.int32) * MOE_BLOCK, side='right'),
        N_EXPERTS - 1).astype(jnp.int32)
    x_pad = jnp.concatenate([xn, jnp.zeros((1, D), xn.dtype)], axis=0)

    def run_block(args):
        tok, e = args
        xb = x_pad[tok]
        hgu = xb @ w_gate_up[e] + b_gate_up[e]
        g = jnp.minimum(hgu[:, :D_EXPERT], SWIGLU_LIMIT)
        up = jnp.clip(hgu[:, D_EXPERT:], -SWIGLU_LIMIT, SWIGLU_LIMIT)
        act = (up + 1.0) * (g * jax.nn.sigmoid(SWIGLU_ALPHA * g))
        return act @ w_down[e] + b_down[e]

    out = lax.map(run_block, (tok_buf.reshape(n_blocks, MOE_BLOCK), blk_exp))
    out = out.reshape(-1, D) * gate_buf[:, None].astype(out.dtype)
    y = jnp.zeros((T + 1, D), out.dtype).at[tok_buf].add(out)
    return y[:T]


def _layer(h, layer, norm_mix_w, w_in, lambda_q1, lambda_k1, lambda_q2, lambda_k2, attn_subln_w,
           w_attn_branch, w_fourier_branch, w_out, norm_moe_w, w_router, b_router,
           w_gate_up, b_gate_up, w_down, b_down):
    B, L, D = h.shape
    xn = _rmsnorm(h, norm_mix_w)
    proj = xn @ w_in
    o1 = ATTN_QK_WIDTH
    o2 = 2 * ATTN_QK_WIDTH
    o3 = o2 + ATTN_V_WIDTH
    o4 = o3 + FOURIER_WIDTH
    q, k, v, u, gl = jnp.split(proj, [o1, o2, o3, o4], axis=-1)
    q = q.reshape(B, L, N_HEADS, 2, HEAD_DIM)
    k = k.reshape(B, L, N_HEADS, 2, HEAD_DIM)
    v = v.reshape(B, L, N_HEADS, 2 * HEAD_DIM)
    cos, sin = _rope_tables(L)
    q = _apply_rope(q, cos, sin)
    k = _apply_rope(k, cos, sin)
    lam_init = _lambda_init(layer)
    lam = (jnp.exp(jnp.sum(lambda_q1.astype(jnp.float32) * lambda_k1.astype(jnp.float32)))
           - jnp.exp(jnp.sum(lambda_q2.astype(jnp.float32) * lambda_k2.astype(jnp.float32)))
           + lam_init)
    attn = _diff_attention(q, k, v, lam)
    attn = _rmsnorm(attn, attn_subln_w) * (1.0 - lam_init)
    a_branch = attn.reshape(B, L, ATTN_V_WIDTH) @ w_attn_branch
    f_branch = _fourier_mix(u) @ w_fourier_branch
    g = jax.nn.sigmoid(gl.astype(jnp.float32)).astype(h.dtype).reshape(B, L, N_BRANCHES, D)
    merged = g[:, :, 0] * a_branch + g[:, :, 1] * f_branch
    h = h + merged @ w_out
    xn2 = _rmsnorm(h, norm_moe_w)
    h = h + _moe(xn2.reshape(B * L, D), w_router, b_router, w_gate_up, b_gate_up,
                 w_down, b_down).reshape(B, L, D)
    return h


def _encode(x, meta_tokens, layer_params, final_norm_w):
    B = x.shape[0]
    meta = jnp.broadcast_to(meta_tokens.astype(x.dtype)[None], (B, N_META, x.shape[-1]))
    h = jnp.concatenate([meta, x], axis=1)
    for layer in range(DEPTH):
        params_l = [p[layer] for p in layer_params]
        h = _layer(h, layer, *params_l)
    h = _rmsnorm(h, final_norm_w)
    return h[:, N_META:]


def setup_inputs(seed: int = 0) -> dict:
    key = jax.random.key(seed)
    ks = jax.random.split(key, 21)
    f32 = jnp.float32

    def nrm(k, shape, scale):
        return jax.random.normal(k, shape, f32) * scale

    D = D_MODEL
    return {
        "x_prompt": nrm(ks[0], (BATCH, SEQ, D), 1.0),
        "x_sample": nrm(ks[1], (DEC_BATCH, DEC_SEQ, D), 1.0),
        "meta_tokens": nrm(ks[2], (N_META, D), 1.0),
        "norm_mix_w": 1.0 + nrm(ks[3], (DEPTH, D), 0.02),
        "w_in": nrm(ks[4], (DEPTH, D, IN_WIDTH), D ** -0.5),
        "lambda_q1": nrm(ks[5], (DEPTH, HEAD_DIM), 0.1),
        "lambda_k1": nrm(ks[6], (DEPTH, HEAD_DIM), 0.1),
        "lambda_q2": nrm(ks[7], (DEPTH, HEAD_DIM), 0.1),
        "lambda_k2": nrm(ks[8], (DEPTH, HEAD_DIM), 0.1),
        "attn_subln_w": 1.0 + nrm(ks[9], (DEPTH, 2 * HEAD_DIM), 0.02),
        "w_attn_branch": nrm(ks[10], (DEPTH, ATTN_V_WIDTH, D), ATTN_V_WIDTH ** -0.5),
        "w_fourier_branch": nrm(ks[11], (DEPTH, FOURIER_WIDTH, D), FOURIER_WIDTH ** -0.5),
        "w_out": nrm(ks[12], (DEPTH, D, D), D ** -0.5),
        "norm_moe_w": 1.0 + nrm(ks[13], (DEPTH, D), 0.02),
        "w_router": nrm(ks[14], (DEPTH, D, N_EXPERTS), D ** -0.5),
        "b_router": nrm(ks[15], (DEPTH, N_EXPERTS), 0.01),
        "w_gate_up": nrm(ks[16], (DEPTH, N_EXPERTS, D, 2 * D_EXPERT), D ** -0.5),
        "b_gate_up": nrm(ks[17], (DEPTH, N_EXPERTS, 2 * D_EXPERT), 0.01),
        "w_down": nrm(ks[18], (DEPTH, N_EXPERTS, D_EXPERT, D), D_EXPERT ** -0.5),
        "b_down": nrm(ks[19], (DEPTH, N_EXPERTS, D), 0.01),
        "final_norm_w": 1.0 + nrm(ks[20], (D,), 0.02),
    }


def reference(x_prompt, x_sample, meta_tokens, norm_mix_w, w_in, lambda_q1, lambda_k1, lambda_q2,
              lambda_k2, attn_subln_w, w_attn_branch, w_fourier_branch, w_out, norm_moe_w,
              w_router, b_router, w_gate_up, b_gate_up, w_down, b_down, final_norm_w):
    layer_params = (norm_mix_w, w_in, lambda_q1, lambda_k1, lambda_q2, lambda_k2, attn_subln_w,
                    w_attn_branch, w_fourier_branch, w_out, norm_moe_w, w_router, b_router,
                    w_gate_up, b_gate_up, w_down, b_down)
    y_prompt = _encode(x_prompt, meta_tokens, layer_params, final_norm_w)
    y_sample = _encode(x_sample, meta_tokens, layer_params, final_norm_w)
    return (y_prompt, y_sample)
```

```python
import functools
import math

import jax
import jax.numpy as jnp
from jax import lax
from jax.experimental import pallas as pl
from jax.experimental.pallas import tpu as pltpu

F32 = jnp.float32
BF16 = jnp.bfloat16

N_META = 16
HEAD_DIM = 128
N_FOURIER_GROUPS = 4
ROPE_THETA = 10000.0
TOP_K = 4
SWIGLU_LIMIT = 7.0
SWIGLU_ALPHA = 1.702
EPS = 1e-6
LANES = 128
NEG_BIG = -1e30
VMEM_LIMIT = 56 * 1024 * 1024


def _cparams(sem, vmem=VMEM_LIMIT, **kw):
    return pltpu.CompilerParams(dimension_semantics=sem, vmem_limit_bytes=vmem, **kw)


def _pick(n, pref):
    t = min(pref, n)
    while n % t:
        t //= 2
    return t


def _mm_kernel(a_ref, b_ref, o_ref, *, precision):
    o_ref[...] = jnp.dot(a_ref[...], b_ref[...], preferred_element_type=F32,
                         precision=precision).astype(o_ref.dtype)


def _matmul(a, b, out_dtype, *, tm, tn, precision=None, name):
    m, k = a.shape
    n = b.shape[1]
    return pl.pallas_call(
        functools.partial(_mm_kernel, precision=precision),
        out_shape=jax.ShapeDtypeStruct((m, n), out_dtype),
        grid=(m // tm, n // tn),
        in_specs=[pl.BlockSpec((tm, k), lambda i, j: (i, 0)),
                  pl.BlockSpec((k, tn), lambda i, j: (0, j))],
        out_specs=pl.BlockSpec((tm, tn), lambda i, j: (i, j)),
        compiler_params=_cparams(("parallel", "arbitrary")),
        name=name,
    )(a, b)


def _inproj_kernel(x_ref, nw_ref, w_ref, cos_ref, sin_ref, o_ref, xn_ref, *, nq, nk, jg, tn,
                   scale):
    j = pl.program_id(1)

    @pl.when(j == 0)
    def _():
        x = x_ref[...]
        ms = jnp.mean(x * x, axis=-1, keepdims=True)
        xn_ref[...] = (x * lax.rsqrt(ms + EPS) * nw_ref[...]).astype(BF16)

    acc = jnp.dot(xn_ref[...], w_ref[...], preferred_element_type=F32)

    def rope_store(s):
        cos = cos_ref[...]
        sin = sin_ref[...]
        for c in range(tn // HEAD_DIM):
            xc = acc[:, c * HEAD_DIM:(c + 1) * HEAD_DIM]
            rc = pltpu.roll(xc, HEAD_DIM // 2, 1)
            y = xc * cos + rc * sin
            if s != 1.0:
                y = y * s
            o_ref[:, c * HEAD_DIM:(c + 1) * HEAD_DIM] = y.astype(o_ref.dtype)

    @pl.when(j < nq)
    def _():
        rope_store(scale)

    @pl.when((j >= nq) & (j < nq + nk))
    def _():
        rope_store(1.0)

    @pl.when((j >= nq + nk) & (j < jg))
    def _():
        o_ref[...] = acc.astype(o_ref.dtype)

    @pl.when(j >= jg)
    def _():
        o_ref[...] = jax.nn.sigmoid(acc).astype(o_ref.dtype)


def _inproj(x, norm_w, w_ext, cos, sin_signed, *, qkw, gate_off, tm, tn, name):
    t, d = x.shape
    nw = w_ext.shape[1]
    kern = functools.partial(_inproj_kernel, nq=qkw // tn, nk=qkw // tn, jg=gate_off // tn,
                             tn=tn, scale=HEAD_DIM ** -0.5)
    return pl.pallas_call(
        kern,
        out_shape=jax.ShapeDtypeStruct((t, nw), BF16),
        grid=(t // tm, nw // tn),
        in_specs=[pl.BlockSpec((tm, d), lambda i, j: (i, 0)),
                  pl.BlockSpec((1, d), lambda i, j: (0, 0)),
                  pl.BlockSpec((d, tn), lambda i, j: (0, j)),
                  pl.BlockSpec((tm, HEAD_DIM), lambda i, j: (i, 0)),
                  pl.BlockSpec((tm, HEAD_DIM), lambda i, j: (i, 0))],
        out_specs=pl.BlockSpec((tm, tn), lambda i, j: (i, j)),
        scratch_shapes=[pltpu.VMEM((tm, d), BF16)],
        compiler_params=_cparams(("parallel", "arbitrary")),
        name=name,
    )(x, norm_w.reshape(1, d), w_ext, cos, sin_signed)


def _attn_kernel(q_ref, k_ref, v_ref, km_ref, vm_ref, lq1_ref, lk1_ref, lq2_ref, lk2_ref,
                 sw_ref, o_ref, *, lam_init):
    d = HEAD_DIM
    lam = (jnp.exp(jnp.sum(lq1_ref[...] * lk1_ref[...], axis=-1, keepdims=True))
           - jnp.exp(jnp.sum(lq2_ref[...] * lk2_ref[...], axis=-1, keepdims=True))
           + lam_init)
    nt = (((1,), (1,)), ((), ()))
    tq = q_ref.shape[0]
    mcols = km_ref.shape[0]
    meta_bias = jnp.where(lax.broadcasted_iota(jnp.int32, (tq, mcols), 1) < N_META, 0.0, NEG_BIG)

    def softmax_parts(m):
        q = q_ref[:, m * d:(m + 1) * d]
        s = lax.dot_general(q, k_ref[:, m * d:(m + 1) * d], nt, preferred_element_type=F32)
        sm = lax.dot_general(q, km_ref[:, m * d:(m + 1) * d], nt,
                             preferred_element_type=F32) + meta_bias
        mx = jnp.maximum(jnp.max(s, axis=-1, keepdims=True), jnp.max(sm, axis=-1, keepdims=True))
        p = jnp.exp(s - mx)
        pm = jnp.exp(sm - mx)
        l = jnp.sum(p, axis=-1, keepdims=True) + jnp.sum(pm, axis=-1, keepdims=True)
        return p, pm, 1.0 / l

    p1, pm1, r1 = softmax_parts(0)
    p2, pm2, r2 = softmax_parts(1)
    r2 = r2 * lam
    w = (p1 * r1 - p2 * r2).astype(BF16)
    wm = (pm1 * r1 - pm2 * r2).astype(BF16)
    o = (jnp.dot(w, v_ref[...], preferred_element_type=F32)
         + jnp.dot(wm, vm_ref[...], preferred_element_type=F32))
    ms = jnp.mean(o * o, axis=-1, keepdims=True)
    o = o * lax.rsqrt(ms + EPS) * sw_ref[...] * (1.0 - lam_init)
    o_ref[...] = o.astype(o_ref.dtype)


def _attention(proj, proj_meta, lam_vecs, subln_w, *, row0, nb, s, n_heads, qkw, lam_init, tq,
               name):
    hw = 2 * HEAD_DIM
    kcol = qkw // hw
    vcol = 2 * qkw // hw
    qb = row0 // tq
    kb = row0 // s
    mrows = proj_meta.shape[0]
    vec = lambda: pl.BlockSpec((1, HEAD_DIM), lambda b, h, i: (0, 0))
    in_specs = [pl.BlockSpec((tq, hw), lambda b, h, i: (qb + b * (s // tq) + i, h)),
                pl.BlockSpec((s, hw), lambda b, h, i: (kb + b, kcol + h)),
                pl.BlockSpec((s, hw), lambda b, h, i: (kb + b, vcol + h)),
                pl.BlockSpec((mrows, hw), lambda b, h, i: (0, kcol + h)),
                pl.BlockSpec((mrows, hw), lambda b, h, i: (0, vcol + h)),
                vec(), vec(), vec(), vec(),
                pl.BlockSpec((1, hw), lambda b, h, i: (0, 0))]
    args = [proj, proj, proj, proj_meta, proj_meta, *lam_vecs, subln_w.reshape(1, hw)]
    return pl.pallas_call(
        functools.partial(_attn_kernel, lam_init=lam_init),
        out_shape=jax.ShapeDtypeStruct((nb * s, n_heads * hw), BF16),
        grid=(nb, n_heads, s // tq),
        in_specs=in_specs,
        out_specs=pl.BlockSpec((tq, hw), lambda b, h, i: (b * (s // tq) + i, h)),
        compiler_params=_cparams(("parallel", "parallel", "arbitrary")),
        name=name,
    )(*args)


def _fourier_kernel(ac_ref, as_ref, uc_ref, us_ref, mc_ref, msn_ref, ucm_ref, usm_ref, o_ref):
    acc = jnp.dot(ac_ref[...], uc_ref[...], preferred_element_type=F32)
    acc += jnp.dot(as_ref[...], us_ref[...], preferred_element_type=F32)
    acc += jnp.dot(mc_ref[...], ucm_ref[...], preferred_element_type=F32)
    acc += jnp.dot(msn_ref[...], usm_ref[...], preferred_element_type=F32)
    o_ref[...] = acc.astype(o_ref.dtype)


def _fourier(proj, proj_meta, tabs, *, row0, nb, s, fw, uc_off, tm, tn, name):
    a_c, a_s, m_c, m_s = tabs
    mrows = proj_meta.shape[0]
    sb = row0 // s
    ucb = uc_off // tn
    usb = (uc_off + fw) // tn
    in_specs = [pl.BlockSpec((tm, s), lambda b, i, j: (i, 0)),
                pl.BlockSpec((tm, s), lambda b, i, j: (i, 0)),
                pl.BlockSpec((s, tn), lambda b, i, j: (sb + b, ucb + j)),
                pl.BlockSpec((s, tn), lambda b, i, j: (sb + b, usb + j)),
                pl.BlockSpec((tm, mrows), lambda b, i, j: (i, 0)),
                pl.BlockSpec((tm, mrows), lambda b, i, j: (i, 0)),
                pl.BlockSpec((mrows, tn), lambda b, i, j: (0, ucb + j)),
                pl.BlockSpec((mrows, tn), lambda b, i, j: (0, usb + j))]
    args = [a_c, a_s, proj, proj, m_c, m_s, proj_meta, proj_meta]
    return pl.pallas_call(
        _fourier_kernel,
        out_shape=jax.ShapeDtypeStruct((nb * s, fw), BF16),
        grid=(nb, s // tm, fw // tn),
        in_specs=in_specs,
        out_specs=pl.BlockSpec((tm, tn), lambda b, i, j: (b * (s // tm) + i, j)),
        compiler_params=_cparams(("parallel", "parallel", "arbitrary")),
        name=name,
    )(*args)


def _dft_tables(s):
    length = s + N_META
    pos_r = jnp.arange(s, dtype=jnp.int32) + N_META
    pos_m = jnp.where(jnp.arange(LANES) < N_META, jnp.arange(LANES, dtype=jnp.int32), 0)
    w = 2.0 * math.pi / length
    nrm = 1.0 / math.sqrt(length)

    def tab(pc, mask=None):
        ang = ((pos_r[:, None] * pc[None, :]) % length).astype(F32) * w
        c = jnp.cos(ang) * nrm
        sn = jnp.sin(ang) * nrm
        if mask is not None:
            c = jnp.where(mask[None, :], c, 0.0)
            sn = jnp.where(mask[None, :], sn, 0.0)
        return c.astype(BF16), sn.astype(BF16)

    a_c, a_s = tab(pos_r)
    m_c, m_s = tab(pos_m, jnp.arange(LANES) < N_META)
    return a_c, a_s, m_c, m_s


def _merge_kernel(a0_ref, a1_ref, f0_ref, f1_ref, wa_ref, wf_ref, g0_ref, g1_ref, o_ref, *, n0):
    def body(a_ref, f_ref):
        ab = jnp.dot(a_ref[...], wa_ref[...], preferred_element_type=F32)
        fb = jnp.dot(f_ref[...], wf_ref[...], preferred_element_type=F32)
        m = g0_ref[...].astype(F32) * ab + g1_ref[...].astype(F32) * fb
        o_ref[...] = m.astype(o_ref.dtype)

    i = pl.program_id(0)

    @pl.when(i < n0)
    def _():
        body(a0_ref, f0_ref)

    @pl.when(i >= n0)
    def _():
        body(a1_ref, f1_ref)


def _merge(attns, fmixes, w_ab, w_fb, proj, *, gate_off, tm, tn):
    (a0, a1), (f0, f1) = attns, fmixes
    vw, fw = a0.shape[1], f0.shape[1]
    d = w_ab.shape[1]
    n0 = a0.shape[0] // tm
    n1 = a1.shape[0] // tm
    g0b = gate_off // tn
    g1b = (gate_off + d) // tn
    first = lambda i, j: (jnp.minimum(i, n0 - 1), 0)
    second = lambda i, j: (jnp.maximum(i - n0, 0), 0)
    return pl.pallas_call(
        functools.partial(_merge_kernel, n0=n0),
        out_shape=jax.ShapeDtypeStruct(((n0 + n1) * tm, d), BF16),
        grid=(n0 + n1, d // tn),
        in_specs=[pl.BlockSpec((tm, vw), first),
                  pl.BlockSpec((tm, vw), second),
                  pl.BlockSpec((tm, fw), first),
                  pl.BlockSpec((tm, fw), second),
                  pl.BlockSpec((vw, tn), lambda i, j: (0, j)),
                  pl.BlockSpec((fw, tn), lambda i, j: (0, j)),
                  pl.BlockSpec((tm, tn), lambda i, j: (i, g0b + j)),
                  pl.BlockSpec((tm, tn), lambda i, j: (i, g1b + j))],
        out_specs=pl.BlockSpec((tm, tn), lambda i, j: (i, j)),
        compiler_params=_cparams(("parallel", "arbitrary")),
        name="merge",
    )(a0, a1, f0, f1, w_ab, w_fb, proj, proj)


def _outproj_kernel(m_ref, w_ref, h_ref, o_ref):
    o_ref[...] = h_ref[...] + jnp.dot(m_ref[...], w_ref[...], preferred_element_type=F32)


def _outproj(merged, w_out, h, *, tm, tn):
    t, d = merged.shape
    return pl.pallas_call(
        _outproj_kernel,
        out_shape=jax.ShapeDtypeStruct((t, d), F32),
        grid=(t // tm, d // tn),
        in_specs=[pl.BlockSpec((tm, d), lambda i, j: (i, 0)),
                  pl.BlockSpec((d, tn), lambda i, j: (0, j)),
                  pl.BlockSpec((tm, tn), lambda i, j: (i, j))],
        out_specs=pl.BlockSpec((tm, tn), lambda i, j: (i, j)),
        compiler_params=_cparams(("parallel", "arbitrary")),
        name="outproj",
    )(merged, w_out, h)


def _router_kernel(h_ref, nw_ref, wr_ref, br_ref, o_ref, cnt_ref, carry_ref, *, top_k):
    i = pl.program_id(0)

    @pl.when(i == 0)
    def _():
        carry_ref[...] = jnp.zeros_like(carry_ref)

    x = h_ref[...]
    tr = x.shape[0]
    ms = jnp.mean(x * x, axis=-1, keepdims=True)
    xn = x * lax.rsqrt(ms + EPS) * nw_ref[...]
    logits = jnp.dot(xn, wr_ref[...], preferred_element_type=F32,
                     precision=lax.Precision.HIGHEST) + br_ref[...]
    lane = lax.broadcasted_iota(jnp.int32, logits.shape, 1)
    work = logits
    idxs, vals = [], []
    sel = jnp.zeros(logits.shape, F32)
    for _ in range(top_k):
        mx = jnp.max(work, axis=-1, keepdims=True)
        ix = jnp.min(jnp.where(work == mx, lane, LANES), axis=-1, keepdims=True)
        hit = lane == ix
        sel = jnp.where(hit, 1.0, sel)
        work = jnp.where(hit, -jnp.inf, work)
        idxs.append(ix)
        vals.append(mx)
    ex = [jnp.exp(v - vals[0]) for v in vals]
    den = ex[0]
    for e in ex[1:]:
        den = den + e
    inv = 1.0 / den
    r_i = lax.broadcasted_iota(jnp.int32, (tr, tr), 0)
    c_i = lax.broadcasted_iota(jnp.int32, (tr, tr), 1)
    tril = jnp.where(c_i < r_i, 1.0, 0.0).astype(BF16)
    rank_dense = jnp.dot(tril, sel.astype(BF16), preferred_element_type=F32) + carry_ref[...]
    out = jnp.zeros(logits.shape, F32)
    for k in range(top_k):
        rk = jnp.sum(jnp.where(lane == idxs[k], rank_dense, 0.0), axis=-1, keepdims=True)
        out = jnp.where(lane == k, idxs[k].astype(F32), out)
        out = jnp.where(lane == top_k + k, ex[k] * inv, out)
        out = jnp.where(lane == 2 * top_k + k, rk, out)
    o_ref[...] = out
    carry_ref[...] += jnp.sum(sel, axis=0, keepdims=True)
    cnt_ref[...] = carry_ref[...]


def _router(h1, norm_w, w_router, b_router, *, tr):
    t, d = h1.shape
    e = w_router.shape[1]
    wr = jnp.zeros((d, LANES), F32).at[:, :e].set(w_router)
    br = jnp.full((1, LANES), NEG_BIG, F32).at[0, :e].set(b_router)
    return pl.pallas_call(
        functools.partial(_router_kernel, top_k=TOP_K),
        out_shape=(jax.ShapeDtypeStruct((t, LANES), F32),
                   jax.ShapeDtypeStruct((1, LANES), F32)),
        grid=(t // tr,),
        in_specs=[pl.BlockSpec((tr, d), lambda i: (i, 0)),
                  pl.BlockSpec((1, d), lambda i: (0, 0)),
                  pl.BlockSpec((d, LANES), lambda i: (0, 0)),
                  pl.BlockSpec((1, LANES), lambda i: (0, 0))],
        out_specs=(pl.BlockSpec((tr, LANES), lambda i: (i, 0)),
                   pl.BlockSpec((1, LANES), lambda i: (0, 0))),
        scratch_shapes=[pltpu.VMEM((1, LANES), F32)],
        compiler_params=_cparams(("arbitrary",)),
        name="router",
    )(h1, norm_w.reshape(1, d), wr, br)


def _gather_rows(idx_ref, src_hbm, dst_ref, sem, n):
    def issue(r, c):
        pltpu.make_async_copy(src_hbm.at[pl.ds(idx_ref[r], 1)], dst_ref.at[pl.ds(r, 1)],
                              sem).start()
        return c

    lax.fori_loop(0, n, issue, 0, unroll=8)
    pltpu.make_async_copy(src_hbm.at[pl.ds(0, n)], dst_ref.at[pl.ds(0, n)], sem).wait()


def _moe_kernel(te_ref, tv_ref, tok_hbm, h_hbm, nw_ref, gate_ref, wg_ref, wu_ref, bg_ref,
                bu_ref, wd_ref, bd_ref, o_ref, tok_smem, xbuf, xn_ref, act_ref, sem_i, sem_g, *,
                nf, tm):
    i = pl.program_id(0)
    s = pl.program_id(1)
    valid = tv_ref[i] != 0

    @pl.when(valid & (s == 0))
    def _():
        cp = pltpu.make_async_copy(tok_hbm.at[i], tok_smem, sem_i)
        cp.start()
        cp.wait()
        _gather_rows(tok_smem.at[0], h_hbm, xbuf, sem_g, tm)
        x = xbuf[...]
        ms = jnp.mean(x * x, axis=-1, keepdims=True)
        xn_ref[...] = (x * lax.rsqrt(ms + EPS) * nw_ref[...]).astype(BF16)

    @pl.when(valid & (s < nf))
    def _():
        xn = xn_ref[...]
        hg = jnp.dot(xn, wg_ref[0], preferred_element_type=F32) + bg_ref[0]
        hu = jnp.dot(xn, wu_ref[0], preferred_element_type=F32) + bu_ref[0]
        g = jnp.minimum(hg, SWIGLU_LIMIT)
        u = jnp.clip(hu, -SWIGLU_LIMIT, SWIGLU_LIMIT)
        act = (u + 1.0) * (g * jax.nn.sigmoid(SWIGLU_ALPHA * g))
        act_ref[jnp.minimum(s, nf - 1)] = act.astype(BF16)

    @pl.when(valid & (s >= nf))
    def _():
        tf = act_ref.shape[2]
        acc = jnp.dot(act_ref[0], wd_ref[0, 0:tf, :], preferred_element_type=F32)
        for f in range(1, nf):
            acc += jnp.dot(act_ref[f], wd_ref[0, f * tf:(f + 1) * tf, :],
                           preferred_element_type=F32)
        o_ref[...] = (acc + bd_ref[0]) * gate_ref[...]

    @pl.when(jnp.logical_not(valid) & (s >= nf))
    def _():
        o_ref[...] = jnp.zeros_like(o_ref)


def _moe(h1, norm_w, tok_tiles, gate_col, tile_expert, tile_valid, w_gu, b_gu, w_dn, b_dn, *,
         tm, tf, tn):
    t, d = h1.shape
    n_exp, _, f2 = w_gu.shape
    fdim = f2 // 2
    nf = fdim // tf
    nn = d // tn
    n_tiles = tok_tiles.shape[0]
    last_f = nf - 1
    last_n = nn - 1

    def fi(i, s, te, tv):
        return jnp.where(tv[i] != 0, jnp.minimum(s, last_f), last_f)

    def ni(i, s, te, tv):
        return jnp.where(tv[i] != 0, jnp.maximum(s - nf, 0), last_n)

    def oi(i, s, te, tv):
        return jnp.maximum(s - nf, 0)

    b_gu3 = b_gu.reshape(n_exp, 1, f2)
    b_dn3 = b_dn.reshape(n_exp, 1, d)
    grid_spec = pltpu.PrefetchScalarGridSpec(
        num_scalar_prefetch=2,
        grid=(n_tiles, nf + nn),
        in_specs=[pl.BlockSpec(memory_space=pl.ANY),
                  pl.BlockSpec(memory_space=pl.ANY),
                  pl.BlockSpec((1, d), lambda i, s, te, tv: (0, 0)),
                  pl.BlockSpec((tm, 1), lambda i, s, te, tv: (i, 0)),
                  pl.BlockSpec((1, d, tf), lambda i, s, te, tv: (te[i], 0, fi(i, s, te, tv))),
                  pl.BlockSpec((1, d, tf),
                               lambda i, s, te, tv: (te[i], 0, nf + fi(i, s, te, tv))),
                  pl.BlockSpec((1, 1, tf), lambda i, s, te, tv: (te[i], 0, fi(i, s, te, tv))),
                  pl.BlockSpec((1, 1, tf),
                               lambda i, s, te, tv: (te[i], 0, nf + fi(i, s, te, tv))),
                  pl.BlockSpec((1, fdim, tn), lambda i, s, te, tv: (te[i], 0, ni(i, s, te, tv))),
                  pl.BlockSpec((1, 1, tn), lambda i, s, te, tv: (te[i], 0, ni(i, s, te, tv)))],
        out_specs=pl.BlockSpec((tm, tn), lambda i, s, te, tv: (i, oi(i, s, te, tv))),
        scratch_shapes=[pltpu.SMEM((1, tm), jnp.int32),
                        pltpu.VMEM((tm, d), F32),
                        pltpu.VMEM((tm, d), BF16),
                        pltpu.VMEM((nf, tm, tf), BF16),
                        pltpu.SemaphoreType.DMA,
                        pltpu.SemaphoreType.DMA],
    )
    return pl.pallas_call(
        functools.partial(_moe_kernel, nf=nf, tm=tm),
        out_shape=jax.ShapeDtypeStruct((n_tiles * tm, d), F32),
        grid_spec=grid_spec,
        compiler_params=_cparams(("arbitrary", "arbitrary")),
        name="moe",
    )(tile_expert, tile_valid, tok_tiles, h1, norm_w.reshape(1, d), gate_col, w_gu, w_gu,
      b_gu3, b_gu3, w_dn, b_dn3)


def _combine_kernel(slot_hbm, y_hbm, h_ref, nw_ref, o_ref, slot_smem, buf, sem_i, sem_g, *,
                    top_k, tile0):
    i = pl.program_id(0)
    tq = h_ref.shape[0]
    cp = pltpu.make_async_copy(slot_hbm.at[tile0 + i], slot_smem, sem_i)
    cp.start()
    cp.wait()
    _gather_rows(slot_smem.at[0], y_hbm, buf, sem_g, top_k * tq)
    y = buf[0:tq, :]
    for k in range(1, top_k):
        y = y + buf[k * tq:(k + 1) * tq, :]
    h = h_ref[...] + y
    ms = jnp.mean(h * h, axis=-1, keepdims=True)
    o_ref[...] = h * lax.rsqrt(ms + EPS) * nw_ref[...]


def _combine(slot_tiles, y_sorted, h1, final_w, *, row0, rows, tq, name):
    d = h1.shape[1]
    kern = functools.partial(_combine_kernel, top_k=TOP_K, tile0=row0 // tq)
    return pl.pallas_call(
        kern,
        out_shape=jax.ShapeDtypeStruct((rows, d), F32),
        grid=(rows // tq,),
        in_specs=[pl.BlockSpec(memory_space=pl.ANY),
                  pl.BlockSpec(memory_space=pl.ANY),
                  pl.BlockSpec((tq, d), lambda i: (row0 // tq + i, 0)),
                  pl.BlockSpec((1, d), lambda i: (0, 0))],
        out_specs=pl.BlockSpec((tq, d), lambda i: (i, 0)),
        scratch_shapes=[pltpu.SMEM((1, TOP_K * tq), jnp.int32),
                        pltpu.VMEM((TOP_K * tq, d), F32),
                        pltpu.SemaphoreType.DMA,
                        pltpu.SemaphoreType.DMA],
        compiler_params=_cparams(("arbitrary",)),
        name=name,
    )(slot_tiles, y_sorted, h1, final_w.reshape(1, d))


def _rope_rows(pos):
    inv_freq = 1.0 / (ROPE_THETA ** (jnp.arange(0, HEAD_DIM, 2, dtype=F32) / HEAD_DIM))
    ang = pos.astype(F32)[:, None] * inv_freq[None, :]
    cos = jnp.cos(ang)
    sin = jnp.sin(ang)
    return (jnp.concatenate([cos, cos], axis=-1), jnp.concatenate([-sin, sin], axis=-1))


def _channel_dft(fw):
    gc = fw // N_FOURIER_GROUPS
    idx = jnp.arange(gc, dtype=jnp.int32)
    ang = ((idx[:, None] * idx[None, :]) % gc).astype(F32) * (2.0 * math.pi / gc)
    nrm = 1.0 / math.sqrt(gc)
    eye = jnp.eye(N_FOURIER_GROUPS, dtype=F32)
    return jnp.concatenate([jnp.kron(eye, jnp.cos(ang) * nrm),
                            jnp.kron(eye, -jnp.sin(ang) * nrm)], axis=1)


def _lambda_init(layer):
    return 0.8 - 0.6 * math.exp(-0.3 * layer)


def _layer(h, meta_h, groups, layer, p):
    t, d = h.shape
    (norm_mix_w, w_in, lq1, lk1, lq2, lk2, subln_w, w_ab, w_fb, w_out) = p
    vw, fw = w_ab.shape[0], w_fb.shape[0]
    qkw = (w_in.shape[1] - vw - fw - 2 * d) // 2
    n_heads = vw // (2 * HEAD_DIM)
    u_off = 2 * qkw + vw
    gate_off = u_off + 2 * fw
    lam_init = _lambda_init(layer)

    w_u = w_in[:, u_off:u_off + fw]
    w_ucs = _matmul(w_u, _channel_dft(fw), F32, tm=_pick(d, 512), tn=_pick(2 * fw, 512),
                    precision=lax.Precision.HIGHEST, name="fold")
    w_ext = jnp.concatenate([w_in[:, :u_off].astype(BF16), w_ucs.astype(BF16),
                             w_in[:, u_off + fw:].astype(BF16)], axis=1)

    pos = jnp.concatenate([N_META + (jnp.arange(nb * s, dtype=jnp.int32) % s)
                           for (_, nb, s) in groups])
    cos, sin_s = _rope_rows(pos)
    mt = meta_h.shape[0]
    cos_m, sin_m = _rope_rows(jnp.arange(mt, dtype=jnp.int32))

    tn_in = _pick(math.gcd(qkw, math.gcd(vw, math.gcd(2 * fw, d))), 512)
    proj = _inproj(h, norm_mix_w, w_ext, cos, sin_s, qkw=qkw, gate_off=gate_off,
                   tm=_pick(t, 512), tn=tn_in, name="inproj")
    proj_meta = _inproj(meta_h, norm_mix_w, w_ext, cos_m, sin_m, qkw=qkw, gate_off=gate_off,
                        tm=mt, tn=tn_in, name="inproj_meta")

    lam_vecs = [v.reshape(1, HEAD_DIM).astype(F32) for v in (lq1, lk1, lq2, lk2)]
    attns, fmixes = [], []
    for gi, (row0, nb, s) in enumerate(groups):
        attns.append(_attention(proj, proj_meta, lam_vecs, subln_w, row0=row0, nb=nb, s=s,
                                n_heads=n_heads, qkw=qkw, lam_init=lam_init, tq=_pick(s, 256),
                                name=f"attn{gi}"))
        fmixes.append(_fourier(proj, proj_meta, _dft_tables(s), row0=row0, nb=nb, s=s, fw=fw,
                               uc_off=u_off, tm=_pick(s, 512), tn=_pick(fw, 512),
                               name=f"fourier{gi}"))

    tm_merge = _pick(math.gcd(*[nb * s for (_, nb, s) in groups]), 512)
    merged = _merge(attns, fmixes, w_ab.astype(BF16), w_fb.astype(BF16), proj, gate_off=gate_off,
                    tm=tm_merge, tn=_pick(math.gcd(d, gate_off), 512))
    return _outproj(merged, w_out.astype(BF16), h, tm=_pick(t, 512), tn=_pick(d, 512))


def _moe_block(h1, groups, p, final_norm_w):
    t, d = h1.shape
    norm_moe_w, w_router, b_router, w_gu, b_gu, w_dn, b_dn = p
    n_exp = w_router.shape[1]
    tm = _pick(t, 512)

    rout, cnt = _router(h1, norm_moe_w, w_router, b_router, tr=_pick(t, 256))
    e_idx = rout[:, 0:TOP_K].astype(jnp.int32)
    gates = rout[:, TOP_K:2 * TOP_K]
    rank = rout[:, 2 * TOP_K:3 * TOP_K].astype(jnp.int32)
    counts = cnt[0, :n_exp].astype(jnp.int32)

    padded = ((counts + tm - 1) // tm) * tm
    pends = jnp.cumsum(padded)
    pstarts = pends - padded
    slot = pstarts[e_idx] + rank
    n_tiles = -(-t * TOP_K // tm) + n_exp
    a_pad = n_tiles * tm
    flat_slot = slot.reshape(-1)
    tok_of_slot = jnp.zeros((a_pad,), jnp.int32).at[flat_slot].set(
        jnp.repeat(jnp.arange(t, dtype=jnp.int32), TOP_K), unique_indices=True)
    gate_of_slot = jnp.zeros((a_pad,), F32).at[flat_slot].set(gates.reshape(-1),
                                                               unique_indices=True)
    tile_start = jnp.arange(n_tiles, dtype=jnp.int32) * tm
    tile_valid = (tile_start < pends[-1]).astype(jnp.int32)
    tile_expert = jnp.minimum(jnp.searchsorted(pends, tile_start, side='right'),
                              n_exp - 1).astype(jnp.int32)
    last_e = tile_expert[jnp.maximum(pends[-1] // tm - 1, 0)]
    tile_expert = jnp.where(tile_valid != 0, tile_expert, last_e)

    y_sorted = _moe(h1, norm_moe_w, tok_of_slot.reshape(n_tiles, 1, tm),
                    gate_of_slot.reshape(a_pad, 1), tile_expert, tile_valid,
                    w_gu.astype(BF16), b_gu, w_dn.astype(BF16), b_dn,
                    tm=tm, tf=_pick(w_dn.shape[1], 512), tn=_pick(d, 512))

    tq = _pick(math.gcd(*[nb * s for (_, nb, s) in groups]), 128)
    slot_tiles = slot.reshape(t // tq, tq, TOP_K).transpose(0, 2, 1).reshape(t // tq, 1,
                                                                              TOP_K * tq)
    return [_combine(slot_tiles, y_sorted, h1, final_norm_w, row0=row0, rows=nb * s, tq=tq,
                     name=f"combine{gi}") for gi, (row0, nb, s) in enumerate(groups)]


def kernel(x_prompt, x_sample, meta_tokens, norm_mix_w, w_in, lambda_q1, lambda_k1, lambda_q2,
           lambda_k2, attn_subln_w, w_attn_branch, w_fourier_branch, w_out, norm_moe_w,
           w_router, b_router, w_gate_up, b_gate_up, w_down, b_down, final_norm_w):
    depth = norm_mix_w.shape[0]
    assert depth == 1, "meta rows are dropped after the first layer's mixing stage"
    d = x_prompt.shape[-1]
    bp, sp, _ = x_prompt.shape
    bs, ss, _ = x_sample.shape
    groups = ((0, bp, sp), (bp * sp, bs, ss))
    h = jnp.concatenate([x_prompt.reshape(bp * sp, d), x_sample.reshape(bs * ss, d)], axis=0)
    meta_h = jnp.zeros((LANES, d), F32).at[:N_META].set(meta_tokens.astype(F32))

    mix_p = (norm_mix_w[0], w_in[0], lambda_q1[0], lambda_k1[0], lambda_q2[0], lambda_k2[0],
             attn_subln_w[0], w_attn_branch[0], w_fourier_branch[0], w_out[0])
    h1 = _layer(h, meta_h, groups, 0, mix_p)
    moe_p = (norm_moe_w[0], w_router[0], b_router[0], w_gate_up[0], b_gate_up[0], w_down[0],
             b_down[0])
    y_p, y_s = _moe_block(h1, groups, moe_p, final_norm_w)
    return (y_p.reshape(bp, sp, d), y_s.reshape(bs, ss, d))
```

```python
import functools
import math

import jax
import jax.numpy as jnp
from jax import lax
from jax.experimental import pallas as pl
from jax.experimental.pallas import tpu as pltpu

F32 = jnp.float32
BF16 = jnp.bfloat16

N_META = 16
HEAD_DIM = 128
N_FOURIER_GROUPS = 4
ROPE_THETA = 10000.0
TOP_K = 4
SWIGLU_LIMIT = 7.0
SWIGLU_ALPHA = 1.702
EPS = 1e-6
LANES = 128
NEG_BIG = -1e30
LOG2E = math.log2(math.e)
VMEM_LIMIT = 56 * 1024 * 1024
MOE_VMEM_LIMIT = 60 * 1024 * 1024
MOE_SUB = 256
MOE_NSUB = 4


def _cparams(sem, vmem=VMEM_LIMIT, **kw):
    return pltpu.CompilerParams(dimension_semantics=sem, vmem_limit_bytes=vmem, **kw)


def _pick(n, pref):
    t = min(pref, n)
    while n % t:
        t //= 2
    return t


def _tile_cols(w, tn):
    k, n = w.shape
    return w.reshape(k, n // tn, tn).transpose(1, 0, 2)


def _mm_kernel(a_ref, b_ref, o_ref, *, precision):
    o_ref[...] = jnp.dot(a_ref[...], b_ref[...], preferred_element_type=F32,
                         precision=precision).astype(o_ref.dtype)


def _matmul(a, b, out_dtype, *, tm, tn, precision=None, name):
    m, k = a.shape
    n = b.shape[1]
    return pl.pallas_call(
        functools.partial(_mm_kernel, precision=precision),
        out_shape=jax.ShapeDtypeStruct((m, n), out_dtype),
        grid=(m // tm, n // tn),
        in_specs=[pl.BlockSpec((tm, k), lambda i, j: (i, 0)),
                  pl.BlockSpec((k, tn), lambda i, j: (0, j))],
        out_specs=pl.BlockSpec((tm, tn), lambda i, j: (i, j)),
        compiler_params=_cparams(("parallel", "arbitrary")),
        name=name,
    )(a, b)


def _inproj_kernel(x_ref, nw_ref, w_ref, cos_ref, sin_ref, o_ref, xn_ref, *, nq, nk, jg, tn,
                   scale):
    j = pl.program_id(1)

    @pl.when(j == 0)
    def _():
        x = x_ref[...]
        ms = jnp.mean(x * x, axis=-1, keepdims=True)
        xn_ref[...] = (x * lax.rsqrt(ms + EPS) * nw_ref[...]).astype(BF16)

    acc = jnp.dot(xn_ref[...], w_ref[...], preferred_element_type=F32)

    def rope_store(s):
        cos = cos_ref[...]
        sin = sin_ref[...]
        for c in range(tn // HEAD_DIM):
            xc = acc[:, c * HEAD_DIM:(c + 1) * HEAD_DIM]
            rc = pltpu.roll(xc, HEAD_DIM // 2, 1)
            y = xc * cos + rc * sin
            if s != 1.0:
                y = y * s
            o_ref[:, c * HEAD_DIM:(c + 1) * HEAD_DIM] = y.astype(o_ref.dtype)

    @pl.when(j < nq)
    def _():
        rope_store(scale)

    @pl.when((j >= nq) & (j < nq + nk))
    def _():
        rope_store(1.0)

    @pl.when((j >= nq + nk) & (j < jg))
    def _():
        o_ref[...] = acc.astype(o_ref.dtype)

    @pl.when(j >= jg)
    def _():
        o_ref[...] = jax.nn.sigmoid(acc).astype(o_ref.dtype)


def _inproj(x, norm_w, w_tiles, cos, sin_signed, *, qkw, gate_off, tm, name):
    t, d = x.shape
    nj, _, tn = w_tiles.shape
    kern = functools.partial(_inproj_kernel, nq=qkw // tn, nk=qkw // tn, jg=gate_off // tn,
                             tn=tn, scale=HEAD_DIM ** -0.5 * LOG2E)
    return pl.pallas_call(
        kern,
        out_shape=jax.ShapeDtypeStruct((t, nj * tn), BF16),
        grid=(t // tm, nj),
        in_specs=[pl.BlockSpec((tm, d), lambda i, j: (i, 0), pipeline_mode=pl.Buffered(1)),
                  pl.BlockSpec((1, d), lambda i, j: (0, 0)),
                  pl.BlockSpec((None, d, tn), lambda i, j: (j, 0, 0)),
                  pl.BlockSpec((tm, HEAD_DIM), lambda i, j: (i, 0)),
                  pl.BlockSpec((tm, HEAD_DIM), lambda i, j: (i, 0))],
        out_specs=pl.BlockSpec((tm, tn), lambda i, j: (i, j)),
        scratch_shapes=[pltpu.VMEM((tm, d), BF16)],
        compiler_params=_cparams(("parallel", "arbitrary")),
        name=name,
    )(x, norm_w.reshape(1, d), w_tiles, cos, sin_signed)


def _attn_kernel(q_ref, k_ref, v_ref, km_ref, vm_ref, lq1_ref, lk1_ref, lq2_ref, lk2_ref,
                 sw_ref, prev_ref, o_ref, *, lam_init, kc):
    del prev_ref
    d = HEAD_DIM
    lam = (jnp.exp(jnp.sum(lq1_ref[...] * lk1_ref[...], axis=-1, keepdims=True))
           - jnp.exp(jnp.sum(lq2_ref[...] * lk2_ref[...], axis=-1, keepdims=True))
           + lam_init)
    nt = (((1,), (1,)), ((), ()))
    tq = q_ref.shape[0]
    s_len = k_ref.shape[0]
    mrows = km_ref.shape[0]
    meta_bias = jnp.where(lax.broadcasted_iota(jnp.int32, (tq, mrows), 1) < N_META, 0.0, NEG_BIG)
    chunks = [(km_ref, vm_ref, 0, mrows, True)]
    chunks += [(k_ref, v_ref, c * kc, kc, False) for c in range(s_len // kc)]

    qs = [q_ref[:, m * d:(m + 1) * d] for m in range(2)]
    mx = [None, None]
    l = [None, None]
    acc = [None, None]
    for (kr, vr, off, n, is_meta) in chunks:
        v = vr[off:off + n, :]
        for m in range(2):
            s = lax.dot_general(qs[m], kr[off:off + n, m * d:(m + 1) * d], nt,
                                preferred_element_type=F32)
            if is_meta:
                s = s + meta_bias
            cmx = jnp.max(s, axis=-1, keepdims=True)
            if mx[m] is None:
                mx[m] = cmx
                p = jnp.exp2(s - cmx)
                l[m] = jnp.sum(p, axis=-1, keepdims=True)
                acc[m] = jnp.dot(p.astype(BF16), v, preferred_element_type=F32)
            else:
                mn = jnp.maximum(mx[m], cmx)
                alpha = jnp.exp2(mx[m] - mn)
                p = jnp.exp2(s - mn)
                l[m] = alpha * l[m] + jnp.sum(p, axis=-1, keepdims=True)
                acc[m] = alpha * acc[m] + jnp.dot(p.astype(BF16), v,
                                                  preferred_element_type=F32)
                mx[m] = mn

    o = acc[0] * (1.0 / l[0]) - acc[1] * (lam / l[1])
    ms = jnp.mean(o * o, axis=-1, keepdims=True)
    o = o * lax.rsqrt(ms + EPS) * sw_ref[...] * (1.0 - lam_init)
    o_ref[...] = o.astype(o_ref.dtype)


def _attention(proj, proj_meta, lam_vecs, subln_w, prev, *, row0, nb, s, n_heads, qkw, lam_init,
               tq, kc, name):
    hw = 2 * HEAD_DIM
    kcol = qkw // hw
    vcol = 2 * qkw // hw
    ob = row0 // tq
    mrows = proj_meta.shape[0]
    vec = lambda: pl.BlockSpec((1, HEAD_DIM), lambda b, h, i: (0, 0))
    in_specs = [pl.BlockSpec((tq, hw), lambda b, h, i: (b * (s // tq) + i, h)),
                pl.BlockSpec((s, hw), lambda b, h, i: (b, kcol + h)),
                pl.BlockSpec((s, hw), lambda b, h, i: (b, vcol + h)),
                pl.BlockSpec((mrows, hw), lambda b, h, i: (0, kcol + h)),
                pl.BlockSpec((mrows, hw), lambda b, h, i: (0, vcol + h)),
                vec(), vec(), vec(), vec(),
                pl.BlockSpec((1, hw), lambda b, h, i: (0, 0)),
                pl.BlockSpec(memory_space=pl.ANY)]
    args = [proj, proj, proj, proj_meta, proj_meta, *lam_vecs, subln_w.reshape(1, hw), prev]
    return pl.pallas_call(
        functools.partial(_attn_kernel, lam_init=lam_init, kc=kc),
        out_shape=jax.ShapeDtypeStruct(prev.shape, prev.dtype),
        grid=(nb, n_heads, s // tq),
        in_specs=in_specs,
        out_specs=pl.BlockSpec((tq, hw), lambda b, h, i: (ob + b * (s // tq) + i, h)),
        input_output_aliases={len(args) - 1: 0},
        compiler_params=_cparams(("parallel", "parallel", "arbitrary")),
        name=name,
    )(*args)


def _fourier_kernel(ac_ref, as_ref, uc_ref, us_ref, mc_ref, msn_ref, ucm_ref, usm_ref, prev_ref,
                    o_ref):
    del prev_ref
    acc = jnp.dot(ac_ref[...], uc_ref[...], preferred_element_type=F32)
    acc += jnp.dot(as_ref[...], us_ref[...], preferred_element_type=F32)
    acc += jnp.dot(mc_ref[...], ucm_ref[...], preferred_element_type=F32)
    acc += jnp.dot(msn_ref[...], usm_ref[...], preferred_element_type=F32)
    o_ref[...] = acc.astype(o_ref.dtype)


def _fourier(proj, proj_meta, tabs, prev, *, row0, nb, s, fw, uc_off, tm, tn, name):
    a_c, a_s, m_c, m_s = tabs
    mrows = proj_meta.shape[0]
    ob = row0 // tm
    ucb = uc_off // tn
    usb = (uc_off + fw) // tn
    in_specs = [pl.BlockSpec((tm, s), lambda b, i, j: (i, 0)),
                pl.BlockSpec((tm, s), lambda b, i, j: (i, 0)),
                pl.BlockSpec((s, tn), lambda b, i, j: (b, ucb + j)),
                pl.BlockSpec((s, tn), lambda b, i, j: (b, usb + j)),
                pl.BlockSpec((tm, mrows), lambda b, i, j: (i, 0)),
                pl.BlockSpec((tm, mrows), lambda b, i, j: (i, 0)),
                pl.BlockSpec((mrows, tn), lambda b, i, j: (0, ucb + j)),
                pl.BlockSpec((mrows, tn), lambda b, i, j: (0, usb + j)),
                pl.BlockSpec(memory_space=pl.ANY)]
    args = [a_c, a_s, proj, proj, m_c, m_s, proj_meta, proj_meta, prev]
    return pl.pallas_call(
        _fourier_kernel,
        out_shape=jax.ShapeDtypeStruct(prev.shape, prev.dtype),
        grid=(nb, s // tm, fw // tn),
        in_specs=in_specs,
        out_specs=pl.BlockSpec((tm, tn), lambda b, i, j: (ob + b * (s // tm) + i, j)),
        input_output_aliases={len(args) - 1: 0},
        compiler_params=_cparams(("parallel", "parallel", "arbitrary")),
        name=name,
    )(*args)


def _dft_tables(s):
    length = s + N_META
    pos_r = jnp.arange(s, dtype=jnp.int32) + N_META
    pos_m = jnp.where(jnp.arange(LANES) < N_META, jnp.arange(LANES, dtype=jnp.int32), 0)
    w = 2.0 * math.pi / length
    nrm = 1.0 / math.sqrt(length)

    def tab(pc, mask=None):
        ang = ((pos_r[:, None] * pc[None, :]) % length).astype(F32) * w
        c = jnp.cos(ang) * nrm
        sn = jnp.sin(ang) * nrm
        if mask is not None:
            c = jnp.where(mask[None, :], c, 0.0)
            sn = jnp.where(mask[None, :], sn, 0.0)
        return c.astype(BF16), sn.astype(BF16)

    a_c, a_s = tab(pos_r)
    m_c, m_s = tab(pos_m, jnp.arange(LANES) < N_META)
    return a_c, a_s, m_c, m_s


def _two_source(n0, nj):
    first = lambda col: (lambda i, j: (jnp.minimum(i, n0 - 1),
                                       col(jnp.where(i < n0, j, nj - 1))))
    second = lambda col: (lambda i, j: (jnp.maximum(i - n0, 0), col(jnp.where(i < n0, 0, j))))
    return first, second


def _merge_kernel(a_ref, f_ref, wa_ref, wf_ref, g0a_ref, g1a_ref, g0b_ref, g1b_ref, o_ref, *,
                  n0):
    ab = jnp.dot(a_ref[...], wa_ref[...], preferred_element_type=F32)
    fb = jnp.dot(f_ref[...], wf_ref[...], preferred_element_type=F32)
    i = pl.program_id(0)

    @pl.when(i < n0)
    def _():
        m = g0a_ref[...].astype(F32) * ab + g1a_ref[...].astype(F32) * fb
        o_ref[...] = m.astype(o_ref.dtype)

    @pl.when(i >= n0)
    def _():
        m = g0b_ref[...].astype(F32) * ab + g1b_ref[...].astype(F32) * fb
        o_ref[...] = m.astype(o_ref.dtype)


def _merge(attn, fmix, wa_tiles, wf_tiles, projs, *, gate_off, tm):
    t, vw = attn.shape
    fw = fmix.shape[1]
    nj, _, tn = wa_tiles.shape
    d = nj * tn
    n0 = projs[0].shape[0] // tm
    g0b = gate_off // tn
    g1b = (gate_off + d) // tn
    first, second = _two_source(n0, nj)
    return pl.pallas_call(
        functools.partial(_merge_kernel, n0=n0),
        out_shape=jax.ShapeDtypeStruct((t, d), BF16),
        grid=(t // tm, nj),
        in_specs=[pl.BlockSpec((tm, vw), lambda i, j: (i, 0)),
                  pl.BlockSpec((tm, fw), lambda i, j: (i, 0)),
                  pl.BlockSpec((None, vw, tn), lambda i, j: (j, 0, 0)),
                  pl.BlockSpec((None, fw, tn), lambda i, j: (j, 0, 0)),
                  pl.BlockSpec((tm, tn), first(lambda j: g0b + j)),
                  pl.BlockSpec((tm, tn), first(lambda j: g1b + j)),
                  pl.BlockSpec((tm, tn), second(lambda j: g0b + j)),
                  pl.BlockSpec((tm, tn), second(lambda j: g1b + j))],
        out_specs=pl.BlockSpec((tm, tn), lambda i, j: (i, j)),
        compiler_params=_cparams(("parallel", "arbitrary")),
        name="merge",
    )(attn, fmix, wa_tiles, wf_tiles, projs[0], projs[0], projs[1], projs[1])


def _outproj_kernel(m_ref, w_ref, ha_ref, hb_ref, o_ref, *, n0):
    acc = jnp.dot(m_ref[...], w_ref[...], preferred_element_type=F32)
    i = pl.program_id(0)

    @pl.when(i < n0)
    def _():
        o_ref[...] = ha_ref[...] + acc

    @pl.when(i >= n0)
    def _():
        o_ref[...] = hb_ref[...] + acc


def _outproj(merged, w_tiles, hs, *, tm):
    t, d = merged.shape
    nj, _, tn = w_tiles.shape
    n0 = hs[0].shape[0] // tm
    first, second = _two_source(n0, nj)
    return pl.pallas_call(
        functools.partial(_outproj_kernel, n0=n0),
        out_shape=jax.ShapeDtypeStruct((t, d), F32),
        grid=(t // tm, nj),
        in_specs=[pl.BlockSpec((tm, d), lambda i, j: (i, 0)),
                  pl.BlockSpec((None, d, tn), lambda i, j: (j, 0, 0)),
                  pl.BlockSpec((tm, tn), first(lambda j: j)),
                  pl.BlockSpec((tm, tn), second(lambda j: j))],
        out_specs=pl.BlockSpec((tm, tn), lambda i, j: (i, j)),
        compiler_params=_cparams(("parallel", "arbitrary")),
        name="outproj",
    )(merged, w_tiles, hs[0], hs[1])


def _router_kernel(h_ref, nw_ref, wr_ref, br_ref, o_ref, cnt_ref, carry_ref, *, top_k):
    i = pl.program_id(0)

    @pl.when(i == 0)
    def _():
        carry_ref[...] = jnp.zeros_like(carry_ref)

    x = h_ref[...]
    tr = x.shape[0]
    ms = jnp.mean(x * x, axis=-1, keepdims=True)
    xn = x * lax.rsqrt(ms + EPS) * nw_ref[...]
    logits = jnp.dot(xn, wr_ref[...], preferred_element_type=F32,
                     precision=lax.Precision.HIGHEST) + br_ref[...]
    lane = lax.broadcasted_iota(jnp.int32, logits.shape, 1)
    work = logits
    idxs, vals = [], []
    sel = jnp.zeros(logits.shape, F32)
    for _ in range(top_k):
        mx = jnp.max(work, axis=-1, keepdims=True)
        ix = jnp.min(jnp.where(work == mx, lane, LANES), axis=-1, keepdims=True)
        hit = lane == ix
        sel = jnp.where(hit, 1.0, sel)
        work = jnp.where(hit, -jnp.inf, work)
        idxs.append(ix)
        vals.append(mx)
    ex = [jnp.exp(v - vals[0]) for v in vals]
    den = ex[0]
    for e in ex[1:]:
        den = den + e
    inv = 1.0 / den
    r_i = lax.broadcasted_iota(jnp.int32, (tr, tr), 0)
    c_i = lax.broadcasted_iota(jnp.int32, (tr, tr), 1)
    tril = jnp.where(c_i < r_i, 1.0, 0.0).astype(BF16)
    rank_dense = jnp.dot(tril, sel.astype(BF16), preferred_element_type=F32) + carry_ref[...]
    out = jnp.zeros(logits.shape, F32)
    for k in range(top_k):
        rk = jnp.sum(jnp.where(lane == idxs[k], rank_dense, 0.0), axis=-1, keepdims=True)
        out = jnp.where(lane == k, idxs[k].astype(F32), out)
        out = jnp.where(lane == top_k + k, ex[k] * inv, out)
        out = jnp.where(lane == 2 * top_k + k, rk, out)
    o_ref[...] = out
    carry_ref[...] += jnp.sum(sel, axis=0, keepdims=True)
    cnt_ref[...] = carry_ref[...]


def _router(h1, norm_w, w_router, b_router, *, tr):
    t, d = h1.shape
    e = w_router.shape[1]
    wr = jnp.zeros((d, LANES), F32).at[:, :e].set(w_router)
    br = jnp.full((1, LANES), NEG_BIG, F32).at[0, :e].set(b_router)
    return pl.pallas_call(
        functools.partial(_router_kernel, top_k=TOP_K),
        out_shape=(jax.ShapeDtypeStruct((t, LANES), F32),
                   jax.ShapeDtypeStruct((1, LANES), F32)),
        grid=(t // tr,),
        in_specs=[pl.BlockSpec((tr, d), lambda i: (i, 0)),
                  pl.BlockSpec((1, d), lambda i: (0, 0)),
                  pl.BlockSpec((d, LANES), lambda i: (0, 0)),
                  pl.BlockSpec((1, LANES), lambda i: (0, 0))],
        out_specs=(pl.BlockSpec((tr, LANES), lambda i: (i, 0)),
                   pl.BlockSpec((1, LANES), lambda i: (0, 0))),
        scratch_shapes=[pltpu.VMEM((1, LANES), F32)],
        compiler_params=_cparams(("arbitrary",)),
        name="router",
    )(h1, norm_w.reshape(1, d), wr, br)


def _gather_start(idx_ref, idx0, src_hbm, dst_ref, sem, n):
    def issue(r, c):
        pltpu.make_async_copy(src_hbm.at[pl.ds(idx_ref[idx0 + r], 1)], dst_ref.at[pl.ds(r, 1)],
                              sem).start()
        return c

    lax.fori_loop(0, n, issue, 0, unroll=8)


def _gather_wait(src_hbm, dst_ref, sem, n):
    pltpu.make_async_copy(src_hbm.at[pl.ds(0, n)], dst_ref.at[pl.ds(0, n)], sem).wait()


def _moe_kernel(te_ref, ns_ref, tok_hbm, h_hbm, nw_ref, gate_ref, wg_ref, wu_ref, bg_ref,
                bu_ref, wd_ref, bd_ref, o_ref, tok_smem, xbuf, xn_ref, act_ref, sem_i, sem_g, *,
                nf, sub, nsub_max):
    i = pl.program_id(0)
    s = pl.program_id(1)
    nsub = ns_ref[i]

    @pl.when((s == 0) & (nsub > 0))
    def _():
        cp = pltpu.make_async_copy(tok_hbm.at[i], tok_smem, sem_i)
        cp.start()
        cp.wait()
        idx = tok_smem.at[0]
        _gather_start(idx, 0, h_hbm, xbuf.at[0], sem_g.at[0], sub)

        def body(c, carry):
            slot = c % 2

            @pl.when(c + 1 < nsub)
            def _():
                _gather_start(idx, (c + 1) * sub, h_hbm, xbuf.at[1 - slot], sem_g.at[1 - slot],
                              sub)

            _gather_wait(h_hbm, xbuf.at[slot], sem_g.at[slot], sub)
            x = xbuf[slot]
            ms = jnp.mean(x * x, axis=-1, keepdims=True)
            xn_ref[c] = (x * lax.rsqrt(ms + EPS) * nw_ref[...]).astype(BF16)
            return carry

        lax.fori_loop(0, nsub, body, 0)

    @pl.when(s < nf)
    def _():
        def body(c, carry):
            xn = xn_ref[c]
            hg = jnp.dot(xn, wg_ref[...], preferred_element_type=F32) + bg_ref[...]
            hu = jnp.dot(xn, wu_ref[...], preferred_element_type=F32) + bu_ref[...]
            g = jnp.minimum(hg, SWIGLU_LIMIT)
            u = jnp.clip(hu, -SWIGLU_LIMIT, SWIGLU_LIMIT)
            act = (u + 1.0) * (g * jax.nn.sigmoid(SWIGLU_ALPHA * g))
            act_ref[c * nf + jnp.minimum(s, nf - 1)] = act.astype(BF16)
            return carry

        lax.fori_loop(0, nsub, body, 0)

    @pl.when(s >= nf)
    def _():
        tf = act_ref.shape[2]

        def body(c, carry):
            acc = jnp.dot(act_ref[c * nf], wd_ref[0:tf, :], preferred_element_type=F32)
            for f in range(1, nf):
                acc += jnp.dot(act_ref[c * nf + f], wd_ref[f * tf:(f + 1) * tf, :],
                               preferred_element_type=F32)
            r0 = pl.multiple_of(c * sub, sub)
            o_ref[pl.ds(r0, sub), :] = (acc + bd_ref[...]) * gate_ref[pl.ds(r0, sub), :]
            return carry

        lax.fori_loop(0, nsub, body, 0)

        def zero(c, carry):
            r0 = pl.multiple_of(c * sub, sub)
            o_ref[pl.ds(r0, sub), :] = jnp.zeros((sub, o_ref.shape[1]), o_ref.dtype)
            return carry

        lax.fori_loop(nsub, nsub_max, zero, 0)


def _moe(h1, norm_w, tok_tiles, gate_col, tile_expert, tile_nsub, wgu_tiles, b_gu, wdn_tiles,
         b_dn, *, sub, nsub_max):
    t, d = h1.shape
    n_exp, nf2, _, tf = wgu_tiles.shape
    nf = nf2 // 2
    _, nn, fdim, tn = wdn_tiles.shape
    n_tiles = tok_tiles.shape[0]
    tm = sub * nsub_max
    last_f = nf - 1
    last_n = nn - 1

    def fi(i, s, te, ns):
        return jnp.where(ns[i] != 0, jnp.minimum(s, last_f), last_f)

    def ni(i, s, te, ns):
        return jnp.where(ns[i] != 0, jnp.maximum(s - nf, 0), last_n)

    def oi(i, s, te, ns):
        return jnp.maximum(s - nf, 0)

    b_gu4 = b_gu.reshape(n_exp, nf2, 1, tf)
    b_dn4 = b_dn.reshape(n_exp, nn, 1, tn)
    grid_spec = pltpu.PrefetchScalarGridSpec(
        num_scalar_prefetch=2,
        grid=(n_tiles, nf + nn),
        in_specs=[pl.BlockSpec(memory_space=pl.ANY),
                  pl.BlockSpec(memory_space=pl.ANY),
                  pl.BlockSpec((1, d), lambda i, s, te, ns: (0, 0)),
                  pl.BlockSpec((tm, 1), lambda i, s, te, ns: (i, 0)),
                  pl.BlockSpec((None, None, d, tf),
                               lambda i, s, te, ns: (te[i], fi(i, s, te, ns), 0, 0)),
                  pl.BlockSpec((None, None, d, tf),
                               lambda i, s, te, ns: (te[i], nf + fi(i, s, te, ns), 0, 0)),
                  pl.BlockSpec((None, None, 1, tf),
                               lambda i, s, te, ns: (te[i], fi(i, s, te, ns), 0, 0)),
                  pl.BlockSpec((None, None, 1, tf),
                               lambda i, s, te, ns: (te[i], nf + fi(i, s, te, ns), 0, 0)),
                  pl.BlockSpec((None, None, fdim, tn),
                               lambda i, s, te, ns: (te[i], ni(i, s, te, ns), 0, 0)),
                  pl.BlockSpec((None, None, 1, tn),
                               lambda i, s, te, ns: (te[i], ni(i, s, te, ns), 0, 0))],
        out_specs=pl.BlockSpec((tm, tn), lambda i, s, te, ns: (i, oi(i, s, te, ns))),
        scratch_shapes=[pltpu.SMEM((1, tm), jnp.int32),
                        pltpu.VMEM((2, sub, d), F32),
                        pltpu.VMEM((nsub_max, sub, d), BF16),
                        pltpu.VMEM((nsub_max * nf, sub, tf), BF16),
                        pltpu.SemaphoreType.DMA,
                        pltpu.SemaphoreType.DMA((2,))],
    )
    return pl.pallas_call(
        functools.partial(_moe_kernel, nf=nf, sub=sub, nsub_max=nsub_max),
        out_shape=jax.ShapeDtypeStruct((n_tiles * tm, d), F32),
        grid_spec=grid_spec,
        compiler_params=_cparams(("arbitrary", "arbitrary"), vmem=MOE_VMEM_LIMIT),
        name="moe",
    )(tile_expert, tile_nsub, tok_tiles, h1, norm_w.reshape(1, d), gate_col, wgu_tiles,
      wgu_tiles, b_gu4, b_gu4, wdn_tiles, b_dn4)


def _combine_kernel(slot_hbm, y_hbm, h_ref, nw_ref, o_ref, slot_smem, buf, sem_i, sem_g, *,
                    top_k, tile0):
    i = pl.program_id(0)
    tq = h_ref.shape[0]
    cp = pltpu.make_async_copy(slot_hbm.at[tile0 + i], slot_smem, sem_i)
    cp.start()
    cp.wait()
    _gather_start(slot_smem.at[0], 0, y_hbm, buf, sem_g, top_k * tq)
    _gather_wait(y_hbm, buf, sem_g, top_k * tq)
    y = buf[0:tq, :]
    for k in range(1, top_k):
        y = y + buf[k * tq:(k + 1) * tq, :]
    h = h_ref[...] + y
    ms = jnp.mean(h * h, axis=-1, keepdims=True)
    o_ref[...] = h * lax.rsqrt(ms + EPS) * nw_ref[...]


def _combine(slot_tiles, y_sorted, h1, final_w, *, row0, rows, tq, name):
    d = h1.shape[1]
    kern = functools.partial(_combine_kernel, top_k=TOP_K, tile0=row0 // tq)
    return pl.pallas_call(
        kern,
        out_shape=jax.ShapeDtypeStruct((rows, d), F32),
        grid=(rows // tq,),
        in_specs=[pl.BlockSpec(memory_space=pl.ANY),
                  pl.BlockSpec(memory_space=pl.ANY),
                  pl.BlockSpec((tq, d), lambda i: (row0 // tq + i, 0)),
                  pl.BlockSpec((1, d), lambda i: (0, 0))],
        out_specs=pl.BlockSpec((tq, d), lambda i: (i, 0)),
        scratch_shapes=[pltpu.SMEM((1, TOP_K * tq), jnp.int32),
                        pltpu.VMEM((TOP_K * tq, d), F32),
                        pltpu.SemaphoreType.DMA,
                        pltpu.SemaphoreType.DMA],
        compiler_params=_cparams(("arbitrary",)),
        name=name,
    )(slot_tiles, y_sorted, h1, final_w.reshape(1, d))


def _rope_rows(pos):
    inv_freq = 1.0 / (ROPE_THETA ** (jnp.arange(0, HEAD_DIM, 2, dtype=F32) / HEAD_DIM))
    ang = pos.astype(F32)[:, None] * inv_freq[None, :]
    cos = jnp.cos(ang)
    sin = jnp.sin(ang)
    return (jnp.concatenate([cos, cos], axis=-1), jnp.concatenate([-sin, sin], axis=-1))


def _channel_dft(fw):
    gc = fw // N_FOURIER_GROUPS
    idx = jnp.arange(gc, dtype=jnp.int32)
    ang = ((idx[:, None] * idx[None, :]) % gc).astype(F32) * (2.0 * math.pi / gc)
    nrm = 1.0 / math.sqrt(gc)
    eye = jnp.eye(N_FOURIER_GROUPS, dtype=F32)
    return jnp.concatenate([jnp.kron(eye, jnp.cos(ang) * nrm),
                            jnp.kron(eye, -jnp.sin(ang) * nrm)], axis=1)


def _lambda_init(layer):
    return 0.8 - 0.6 * math.exp(-0.3 * layer)


def _mix_block(xs, meta_h, layer, p):
    (norm_mix_w, w_in, lq1, lk1, lq2, lk2, subln_w, w_ab, w_fb, w_out) = p
    d = w_in.shape[0]
    vw, fw = w_ab.shape[0], w_fb.shape[0]
    qkw = (w_in.shape[1] - vw - fw - 2 * d) // 2
    n_heads = vw // (2 * HEAD_DIM)
    u_off = 2 * qkw + vw
    gate_off = u_off + 2 * fw
    lam_init = _lambda_init(layer)
    rows = [x.shape[0] * x.shape[1] for x in xs]
    t = sum(rows)
    tm = _pick(math.gcd(*rows), 1024)

    w_u = w_in[:, u_off:u_off + fw]
    w_ucs = _matmul(w_u, _channel_dft(fw), F32, tm=_pick(d, 512), tn=_pick(2 * fw, 512),
                    precision=lax.Precision.HIGHEST, name="fold")
    tn_in = _pick(math.gcd(qkw, math.gcd(vw, math.gcd(2 * fw, d))), 512)
    w_ext = jnp.concatenate([w_in[:, :u_off].astype(BF16), w_ucs.astype(BF16),
                             w_in[:, u_off + fw:].astype(BF16)], axis=1)
    w_tiles = _tile_cols(w_ext, tn_in)

    mt = meta_h.shape[0]
    cos_m, sin_m = _rope_rows(jnp.arange(mt, dtype=jnp.int32))
    proj_meta = _inproj(meta_h, norm_mix_w, w_tiles, cos_m, sin_m, qkw=qkw, gate_off=gate_off,
                        tm=mt, name="inproj_meta")

    lam_vecs = [v.reshape(1, HEAD_DIM).astype(F32) for v in (lq1, lk1, lq2, lk2)]
    attn = jnp.zeros((t, vw), BF16)
    fmix = jnp.zeros((t, fw), BF16)
    projs, hs = [], []
    row0 = 0
    for gi, x in enumerate(xs):
        nb, s, _ = x.shape
        h = x.reshape(nb * s, d)
        cos, sin_s = _rope_rows(N_META + (jnp.arange(nb * s, dtype=jnp.int32) % s))
        proj = _inproj(h, norm_mix_w, w_tiles, cos, sin_s, qkw=qkw, gate_off=gate_off, tm=tm,
                       name=f"inproj{gi}")
        attn = _attention(proj, proj_meta, lam_vecs, subln_w, attn, row0=row0, nb=nb, s=s,
                          n_heads=n_heads, qkw=qkw, lam_init=lam_init, tq=_pick(s, 512),
                          kc=_pick(s, 512), name=f"attn{gi}")
        fmix = _fourier(proj, proj_meta, _dft_tables(s), fmix, row0=row0, nb=nb, s=s, fw=fw,
                        uc_off=u_off, tm=_pick(s, 512), tn=_pick(fw, 512), name=f"fourier{gi}")
        projs.append(proj)
        hs.append(h)
        row0 += nb * s

    tn = _pick(math.gcd(d, gate_off), 512)
    merged = _merge(attn, fmix, _tile_cols(w_ab.astype(BF16), tn), _tile_cols(w_fb.astype(BF16), tn),
                    projs, gate_off=gate_off, tm=tm)
    return _outproj(merged, _tile_cols(w_out.astype(BF16), tn), hs, tm=tm)


def _moe_block(h1, rows, p, final_norm_w):
    t, d = h1.shape
    norm_moe_w, w_router, b_router, w_gu, b_gu, w_dn, b_dn = p
    n_exp = w_router.shape[1]
    fdim = w_dn.shape[1]
    sub = _pick(t, MOE_SUB)
    tm = sub * MOE_NSUB

    rout, cnt = _router(h1, norm_moe_w, w_router, b_router, tr=_pick(t, 256))
    e_idx = rout[:, 0:TOP_K].astype(jnp.int32)
    gates = rout[:, TOP_K:2 * TOP_K]
    rank = rout[:, 2 * TOP_K:3 * TOP_K].astype(jnp.int32)
    counts = cnt[0, :n_exp].astype(jnp.int32)

    nsub_e = (counts + sub - 1) // sub
    tiles_e = (nsub_e + MOE_NSUB - 1) // MOE_NSUB
    tile_end = jnp.cumsum(tiles_e)
    tile_base = tile_end - tiles_e
    slot = tile_base[e_idx] * tm + rank
    n_tiles = -(-t * TOP_K // tm) + n_exp
    a_pad = n_tiles * tm
    tok_gate = jnp.stack([jnp.repeat(jnp.arange(t, dtype=jnp.int32), TOP_K).astype(F32),
                          gates.reshape(-1)], axis=1)
    by_slot = jnp.zeros((a_pad, 2), F32).at[slot.reshape(-1)].set(tok_gate, unique_indices=True)
    tok_of_slot = by_slot[:, 0].astype(jnp.int32)
    gate_of_slot = by_slot[:, 1:2]
    tile_id = jnp.arange(n_tiles, dtype=jnp.int32)
    tile_expert = jnp.minimum(jnp.searchsorted(tile_end, tile_id, side='right'),
                              n_exp - 1).astype(jnp.int32)
    tile_nsub = jnp.clip(nsub_e[tile_expert] - (tile_id - tile_base[tile_expert]) * MOE_NSUB,
                         0, MOE_NSUB)
    tile_nsub = jnp.where(tile_id < tile_end[-1], tile_nsub, 0).astype(jnp.int32)
    last_e = tile_expert[jnp.maximum(tile_end[-1] - 1, 0)]
    tile_expert = jnp.where(tile_nsub != 0, tile_expert, last_e)

    tf = _pick(fdim, 512)
    tn = _pick(d, 512)
    wgu_tiles = w_gu.astype(BF16).reshape(n_exp, d, 2 * fdim // tf, tf).transpose(0, 2, 1, 3)
    wdn_tiles = w_dn.astype(BF16).reshape(n_exp, fdim, d // tn, tn).transpose(0, 2, 1, 3)
    y_sorted = _moe(h1, norm_moe_w, tok_of_slot.reshape(n_tiles, 1, tm), gate_of_slot,
                    tile_expert, tile_nsub, wgu_tiles, b_gu, wdn_tiles, b_dn, sub=sub,
                    nsub_max=MOE_NSUB)

    tq = _pick(math.gcd(*rows), 128)
    slot_tiles = slot.reshape(t // tq, tq, TOP_K).transpose(0, 2, 1).reshape(t // tq, 1,
                                                                              TOP_K * tq)
    outs = []
    row0 = 0
    for gi, r in enumerate(rows):
        outs.append(_combine(slot_tiles, y_sorted, h1, final_norm_w, row0=row0, rows=r, tq=tq,
                             name=f"combine{gi}"))
        row0 += r
    return outs


def kernel(x_prompt, x_sample, meta_tokens, norm_mix_w, w_in, lambda_q1, lambda_k1, lambda_q2,
           lambda_k2, attn_subln_w, w_attn_branch, w_fourier_branch, w_out, norm_moe_w,
           w_router, b_router, w_gate_up, b_gate_up, w_down, b_down, final_norm_w):
    depth = norm_mix_w.shape[0]
    assert depth == 1, "meta rows are dropped after the first layer's mixing stage"
    d = x_prompt.shape[-1]
    xs = (x_prompt, x_sample)
    meta_h = jnp.zeros((LANES, d), F32).at[:N_META].set(meta_tokens.astype(F32))

    mix_p = (norm_mix_w[0], w_in[0], lambda_q1[0], lambda_k1[0], lambda_q2[0], lambda_k2[0],
             attn_subln_w[0], w_attn_branch[0], w_fourier_branch[0], w_out[0])
    h1 = _mix_block(xs, meta_h, 0, mix_p)
    moe_p = (norm_moe_w[0], w_router[0], b_router[0], w_gate_up[0], b_gate_up[0], w_down[0],
             b_down[0])
    rows = [x.shape[0] * x.shape[1] for x in xs]
    outs = _moe_block(h1, rows, moe_p, final_norm_w)
    return tuple(o.reshape(x.shape) for o, x in zip(outs, xs))
```

```python
import functools
import math

import jax
import jax.numpy as jnp
from jax import lax
from jax.experimental import pallas as pl
from jax.experimental.pallas import tpu as pltpu

F32 = jnp.float32
BF16 = jnp.bfloat16

N_META = 16
HEAD_DIM = 128
N_FOURIER_GROUPS = 4
ROPE_THETA = 10000.0
TOP_K = 4
SWIGLU_LIMIT = 7.0
SWIGLU_ALPHA = 1.702
EPS = 1e-6
LANES = 128
NEG_BIG = -1e30
LOG2E = math.log2(math.e)
VMEM_LIMIT = 56 * 1024 * 1024
MOE_VMEM_LIMIT = 60 * 1024 * 1024
MOE_SUB = 256
MOE_NSUB = 4


def _cparams(sem, vmem=VMEM_LIMIT, **kw):
    return pltpu.CompilerParams(dimension_semantics=sem, vmem_limit_bytes=vmem, **kw)


def _pick(n, pref):
    t = min(pref, n)
    while n % t:
        t //= 2
    return t


def _mm_kernel(a_ref, b_ref, o_ref, *, precision):
    o_ref[...] = jnp.dot(a_ref[...], b_ref[...], preferred_element_type=F32,
                         precision=precision).astype(o_ref.dtype)


def _matmul(a, b, out_dtype, *, tm, tn, precision=None, name):
    m, k = a.shape
    n = b.shape[1]
    return pl.pallas_call(
        functools.partial(_mm_kernel, precision=precision),
        out_shape=jax.ShapeDtypeStruct((m, n), out_dtype),
        grid=(m // tm, n // tn),
        in_specs=[pl.BlockSpec((tm, k), lambda i, j: (i, 0)),
                  pl.BlockSpec((k, tn), lambda i, j: (0, j))],
        out_specs=pl.BlockSpec((tm, tn), lambda i, j: (i, j)),
        compiler_params=_cparams(("parallel", "arbitrary")),
        name=name,
    )(a, b)


def _inproj_kernel(x_ref, nw_ref, w_ref, cos_ref, sin_ref, o_ref, xn_ref, *, nq, nk, jg, tn,
                   scale):
    j = pl.program_id(1)

    @pl.when(j == 0)
    def _():
        x = x_ref[...]
        ms = jnp.mean(x * x, axis=-1, keepdims=True)
        xn_ref[...] = (x * lax.rsqrt(ms + EPS) * nw_ref[...]).astype(BF16)

    acc = jnp.dot(xn_ref[...], w_ref[...], preferred_element_type=F32)

    def rope_store(s):
        cos = cos_ref[...]
        sin = sin_ref[...]
        for c in range(tn // HEAD_DIM):
            xc = acc[:, c * HEAD_DIM:(c + 1) * HEAD_DIM]
            rc = pltpu.roll(xc, HEAD_DIM // 2, 1)
            y = xc * cos + rc * sin
            if s != 1.0:
                y = y * s
            o_ref[:, c * HEAD_DIM:(c + 1) * HEAD_DIM] = y.astype(o_ref.dtype)

    @pl.when(j < nq)
    def _():
        rope_store(scale)

    @pl.when((j >= nq) & (j < nq + nk))
    def _():
        rope_store(1.0)

    @pl.when((j >= nq + nk) & (j < jg))
    def _():
        o_ref[...] = acc.astype(o_ref.dtype)

    @pl.when(j >= jg)
    def _():
        o_ref[...] = jax.nn.sigmoid(acc).astype(o_ref.dtype)


def _inproj(x, norm_w, w_ext, cos, sin_signed, *, qkw, gate_off, tm, tn, name):
    t, d = x.shape
    nj = w_ext.shape[1] // tn
    kern = functools.partial(_inproj_kernel, nq=qkw // tn, nk=qkw // tn, jg=gate_off // tn,
                             tn=tn, scale=HEAD_DIM ** -0.5 * LOG2E)
    return pl.pallas_call(
        kern,
        out_shape=jax.ShapeDtypeStruct((t, nj * tn), BF16),
        grid=(t // tm, nj),
        in_specs=[pl.BlockSpec((tm, d), lambda i, j: (i, 0), pipeline_mode=pl.Buffered(1)),
                  pl.BlockSpec((1, d), lambda i, j: (0, 0)),
                  pl.BlockSpec((d, tn), lambda i, j: (0, j)),
                  pl.BlockSpec((tm, HEAD_DIM), lambda i, j: (i, 0)),
                  pl.BlockSpec((tm, HEAD_DIM), lambda i, j: (i, 0))],
        out_specs=pl.BlockSpec((tm, tn), lambda i, j: (i, j)),
        scratch_shapes=[pltpu.VMEM((tm, d), BF16)],
        compiler_params=_cparams(("parallel", "arbitrary")),
        name=name,
    )(x, norm_w.reshape(1, d), w_ext, cos, sin_signed)


def _attn_kernel(q_ref, k_ref, v_ref, km_ref, vm_ref, lq1_ref, lk1_ref, lq2_ref, lk2_ref,
                 sw_ref, prev_ref, o_ref, *, lam_init, kc):
    del prev_ref
    d = HEAD_DIM
    lam = (jnp.exp(jnp.sum(lq1_ref[...] * lk1_ref[...], axis=-1, keepdims=True))
           - jnp.exp(jnp.sum(lq2_ref[...] * lk2_ref[...], axis=-1, keepdims=True))
           + lam_init)
    nt = (((1,), (1,)), ((), ()))
    tq = q_ref.shape[0]
    s_len = k_ref.shape[0]
    mrows = km_ref.shape[0]
    meta_bias = jnp.where(lax.broadcasted_iota(jnp.int32, (tq, mrows), 1) < N_META, 0.0, NEG_BIG)
    chunks = [(km_ref, vm_ref, 0, mrows, True)]
    chunks += [(k_ref, v_ref, c * kc, kc, False) for c in range(s_len // kc)]

    qs = [q_ref[:, m * d:(m + 1) * d] for m in range(2)]
    mx = [None, None]
    l = [None, None]
    acc = [None, None]
    for (kr, vr, off, n, is_meta) in chunks:
        v = vr[off:off + n, :]
        for m in range(2):
            s = lax.dot_general(qs[m], kr[off:off + n, m * d:(m + 1) * d], nt,
                                preferred_element_type=F32)
            if is_meta:
                s = s + meta_bias
            cmx = jnp.max(s, axis=-1, keepdims=True)
            if mx[m] is None:
                mx[m] = cmx
                p = jnp.exp2(s - cmx)
                l[m] = jnp.sum(p, axis=-1, keepdims=True)
                acc[m] = jnp.dot(p.astype(BF16), v, preferred_element_type=F32)
            else:
                mn = jnp.maximum(mx[m], cmx)
                alpha = jnp.exp2(mx[m] - mn)
                p = jnp.exp2(s - mn)
                l[m] = alpha * l[m] + jnp.sum(p, axis=-1, keepdims=True)
                acc[m] = alpha * acc[m] + jnp.dot(p.astype(BF16), v,
                                                  preferred_element_type=F32)
                mx[m] = mn

    o = acc[0] * (1.0 / l[0]) - acc[1] * (lam / l[1])
    ms = jnp.mean(o * o, axis=-1, keepdims=True)
    o = o * lax.rsqrt(ms + EPS) * sw_ref[...] * (1.0 - lam_init)
    o_ref[...] = o.astype(o_ref.dtype)


def _attention(proj, proj_meta, lam_vecs, subln_w, prev, *, row0, nb, s, n_heads, qkw, lam_init,
               tq, kc, name):
    hw = 2 * HEAD_DIM
    kcol = qkw // hw
    vcol = 2 * qkw // hw
    ob = row0 // tq
    mrows = proj_meta.shape[0]
    vec = lambda: pl.BlockSpec((1, HEAD_DIM), lambda b, h, i: (0, 0))
    in_specs = [pl.BlockSpec((tq, hw), lambda b, h, i: (b * (s // tq) + i, h)),
                pl.BlockSpec((s, hw), lambda b, h, i: (b, kcol + h)),
                pl.BlockSpec((s, hw), lambda b, h, i: (b, vcol + h)),
                pl.BlockSpec((mrows, hw), lambda b, h, i: (0, kcol + h)),
                pl.BlockSpec((mrows, hw), lambda b, h, i: (0, vcol + h)),
                vec(), vec(), vec(), vec(),
                pl.BlockSpec((1, hw), lambda b, h, i: (0, 0)),
                pl.BlockSpec(memory_space=pl.ANY)]
    args = [proj, proj, proj, proj_meta, proj_meta, *lam_vecs, subln_w.reshape(1, hw), prev]
    return pl.pallas_call(
        functools.partial(_attn_kernel, lam_init=lam_init, kc=kc),
        out_shape=jax.ShapeDtypeStruct(prev.shape, prev.dtype),
        grid=(nb, n_heads, s // tq),
        in_specs=in_specs,
        out_specs=pl.BlockSpec((tq, hw), lambda b, h, i: (ob + b * (s // tq) + i, h)),
        input_output_aliases={len(args) - 1: 0},
        compiler_params=_cparams(("parallel", "parallel", "arbitrary")),
        name=name,
    )(*args)


def _fourier_kernel(ac_ref, as_ref, uc_ref, us_ref, mc_ref, msn_ref, ucm_ref, usm_ref, prev_ref,
                    o_ref):
    del prev_ref
    acc = jnp.dot(ac_ref[...], uc_ref[...], preferred_element_type=F32)
    acc += jnp.dot(as_ref[...], us_ref[...], preferred_element_type=F32)
    acc += jnp.dot(mc_ref[...], ucm_ref[...], preferred_element_type=F32)
    acc += jnp.dot(msn_ref[...], usm_ref[...], preferred_element_type=F32)
    o_ref[...] = acc.astype(o_ref.dtype)


def _fourier(proj, proj_meta, tabs, prev, *, row0, nb, s, fw, uc_off, tm, tn, name):
    a_c, a_s, m_c, m_s = tabs
    mrows = proj_meta.shape[0]
    ob = row0 // tm
    ucb = uc_off // tn
    usb = (uc_off + fw) // tn
    in_specs = [pl.BlockSpec((tm, s), lambda b, i, j: (i, 0)),
                pl.BlockSpec((tm, s), lambda b, i, j: (i, 0)),
                pl.BlockSpec((s, tn), lambda b, i, j: (b, ucb + j)),
                pl.BlockSpec((s, tn), lambda b, i, j: (b, usb + j)),
                pl.BlockSpec((tm, mrows), lambda b, i, j: (i, 0)),
                pl.BlockSpec((tm, mrows), lambda b, i, j: (i, 0)),
                pl.BlockSpec((mrows, tn), lambda b, i, j: (0, ucb + j)),
                pl.BlockSpec((mrows, tn), lambda b, i, j: (0, usb + j)),
                pl.BlockSpec(memory_space=pl.ANY)]
    args = [a_c, a_s, proj, proj, m_c, m_s, proj_meta, proj_meta, prev]
    return pl.pallas_call(
        _fourier_kernel,
        out_shape=jax.ShapeDtypeStruct(prev.shape, prev.dtype),
        grid=(nb, s // tm, fw // tn),
        in_specs=in_specs,
        out_specs=pl.BlockSpec((tm, tn), lambda b, i, j: (ob + b * (s // tm) + i, j)),
        input_output_aliases={len(args) - 1: 0},
        compiler_params=_cparams(("parallel", "parallel", "arbitrary")),
        name=name,
    )(*args)


def _dft_tables(s):
    length = s + N_META
    pos_r = jnp.arange(s, dtype=jnp.int32) + N_META
    pos_m = jnp.where(jnp.arange(LANES) < N_META, jnp.arange(LANES, dtype=jnp.int32), 0)
    w = 2.0 * math.pi / length
    nrm = 1.0 / math.sqrt(length)

    def tab(pc, mask=None):
        ang = ((pos_r[:, None] * pc[None, :]) % length).astype(F32) * w
        c = jnp.cos(ang) * nrm
        sn = jnp.sin(ang) * nrm
        if mask is not None:
            c = jnp.where(mask[None, :], c, 0.0)
            sn = jnp.where(mask[None, :], sn, 0.0)
        return c.astype(BF16), sn.astype(BF16)

    a_c, a_s = tab(pos_r)
    m_c, m_s = tab(pos_m, jnp.arange(LANES) < N_META)
    return a_c, a_s, m_c, m_s


def _two_source(n0, nj):
    first = lambda col: (lambda i, j: (jnp.minimum(i, n0 - 1),
                                       col(jnp.where(i < n0, j, nj - 1))))
    second = lambda col: (lambda i, j: (jnp.maximum(i - n0, 0), col(jnp.where(i < n0, 0, j))))
    return first, second


def _merge_kernel(a_ref, f_ref, wa_ref, wf_ref, g0a_ref, g1a_ref, g0b_ref, g1b_ref, o_ref, *,
                  n0):
    ab = jnp.dot(a_ref[...], wa_ref[...], preferred_element_type=F32)
    fb = jnp.dot(f_ref[...], wf_ref[...], preferred_element_type=F32)
    i = pl.program_id(0)

    @pl.when(i < n0)
    def _():
        m = g0a_ref[...].astype(F32) * ab + g1a_ref[...].astype(F32) * fb
        o_ref[...] = m.astype(o_ref.dtype)

    @pl.when(i >= n0)
    def _():
        m = g0b_ref[...].astype(F32) * ab + g1b_ref[...].astype(F32) * fb
        o_ref[...] = m.astype(o_ref.dtype)


def _merge(attn, fmix, w_ab, w_fb, projs, *, gate_off, tm, tn):
    t, vw = attn.shape
    fw = fmix.shape[1]
    d = w_ab.shape[1]
    nj = d // tn
    n0 = projs[0].shape[0] // tm
    g0b = gate_off // tn
    g1b = (gate_off + d) // tn
    first, second = _two_source(n0, nj)
    return pl.pallas_call(
        functools.partial(_merge_kernel, n0=n0),
        out_shape=jax.ShapeDtypeStruct((t, d), BF16),
        grid=(t // tm, nj),
        in_specs=[pl.BlockSpec((tm, vw), lambda i, j: (i, 0)),
                  pl.BlockSpec((tm, fw), lambda i, j: (i, 0)),
                  pl.BlockSpec((vw, tn), lambda i, j: (0, j)),
                  pl.BlockSpec((fw, tn), lambda i, j: (0, j)),
                  pl.BlockSpec((tm, tn), first(lambda j: g0b + j)),
                  pl.BlockSpec((tm, tn), first(lambda j: g1b + j)),
                  pl.BlockSpec((tm, tn), second(lambda j: g0b + j)),
                  pl.BlockSpec((tm, tn), second(lambda j: g1b + j))],
        out_specs=pl.BlockSpec((tm, tn), lambda i, j: (i, j)),
        compiler_params=_cparams(("parallel", "arbitrary")),
        name="merge",
    )(attn, fmix, w_ab, w_fb, projs[0], projs[0], projs[1], projs[1])


def _outproj_kernel(m_ref, w_ref, ha_ref, hb_ref, o_ref, *, n0):
    acc = jnp.dot(m_ref[...], w_ref[...], preferred_element_type=F32)
    i = pl.program_id(0)

    @pl.when(i < n0)
    def _():
        o_ref[...] = ha_ref[...] + acc

    @pl.when(i >= n0)
    def _():
        o_ref[...] = hb_ref[...] + acc


def _outproj(merged, w_out, hs, *, tm, tn):
    t, d = merged.shape
    nj = d // tn
    n0 = hs[0].shape[0] // tm
    first, second = _two_source(n0, nj)
    return pl.pallas_call(
        functools.partial(_outproj_kernel, n0=n0),
        out_shape=jax.ShapeDtypeStruct((t, d), F32),
        grid=(t // tm, nj),
        in_specs=[pl.BlockSpec((tm, d), lambda i, j: (i, 0)),
                  pl.BlockSpec((d, tn), lambda i, j: (0, j)),
                  pl.BlockSpec((tm, tn), first(lambda j: j)),
                  pl.BlockSpec((tm, tn), second(lambda j: j))],
        out_specs=pl.BlockSpec((tm, tn), lambda i, j: (i, j)),
        compiler_params=_cparams(("parallel", "arbitrary")),
        name="outproj",
    )(merged, w_out, hs[0], hs[1])


def _pack_bf16_pairs(x):
    n = x.shape[1] // 2
    bits = lax.bitcast_convert_type(x.astype(BF16).astype(F32), jnp.uint32)
    return (bits[:, :n] >> 16) | (bits[:, n:] & jnp.uint32(0xFFFF0000))


def _unpack_bf16_pairs(w):
    lo = lax.bitcast_convert_type(w << 16, F32)
    hi = lax.bitcast_convert_type(w & jnp.uint32(0xFFFF0000), F32)
    return lo.astype(BF16), hi.astype(BF16)


def _router_kernel(h_ref, nw_ref, wr_ref, br_ref, o_ref, cnt_ref, xp_ref, carry_ref, *, top_k):
    i = pl.program_id(0)

    @pl.when(i == 0)
    def _():
        carry_ref[...] = jnp.zeros_like(carry_ref)

    x = h_ref[...]
    tr = x.shape[0]
    ms = jnp.mean(x * x, axis=-1, keepdims=True)
    xn = x * lax.rsqrt(ms + EPS) * nw_ref[...]
    xp_ref[...] = _pack_bf16_pairs(xn)
    logits = jnp.dot(xn, wr_ref[...], preferred_element_type=F32,
                     precision=lax.Precision.HIGHEST) + br_ref[...]
    lane = lax.broadcasted_iota(jnp.int32, logits.shape, 1)
    work = logits
    idxs, vals = [], []
    sel = jnp.zeros(logits.shape, F32)
    for _ in range(top_k):
        mx = jnp.max(work, axis=-1, keepdims=True)
        ix = jnp.min(jnp.where(work == mx, lane, LANES), axis=-1, keepdims=True)
        hit = lane == ix
        sel = jnp.where(hit, 1.0, sel)
        work = jnp.where(hit, -jnp.inf, work)
        idxs.append(ix)
        vals.append(mx)
    ex = [jnp.exp(v - vals[0]) for v in vals]
    den = ex[0]
    for e in ex[1:]:
        den = den + e
    inv = 1.0 / den
    r_i = lax.broadcasted_iota(jnp.int32, (tr, tr), 0)
    c_i = lax.broadcasted_iota(jnp.int32, (tr, tr), 1)
    tril = jnp.where(c_i < r_i, 1.0, 0.0).astype(BF16)
    rank_dense = jnp.dot(tril, sel.astype(BF16), preferred_element_type=F32) + carry_ref[...]
    out = jnp.zeros(logits.shape, F32)
    for k in range(top_k):
        rk = jnp.sum(jnp.where(lane == idxs[k], rank_dense, 0.0), axis=-1, keepdims=True)
        out = jnp.where(lane == k, idxs[k].astype(F32), out)
        out = jnp.where(lane == top_k + k, ex[k] * inv, out)
        out = jnp.where(lane == 2 * top_k + k, rk, out)
    o_ref[...] = out
    carry_ref[...] += jnp.sum(sel, axis=0, keepdims=True)
    cnt_ref[...] = carry_ref[...]


def _router(h1, norm_w, w_router, b_router, *, tr):
    t, d = h1.shape
    e = w_router.shape[1]
    wr = jnp.zeros((d, LANES), F32).at[:, :e].set(w_router)
    br = jnp.full((1, LANES), NEG_BIG, F32).at[0, :e].set(b_router)
    return pl.pallas_call(
        functools.partial(_router_kernel, top_k=TOP_K),
        out_shape=(jax.ShapeDtypeStruct((t, LANES), F32),
                   jax.ShapeDtypeStruct((1, LANES), F32),
                   jax.ShapeDtypeStruct((t, d // 2), jnp.uint32)),
        grid=(t // tr,),
        in_specs=[pl.BlockSpec((tr, d), lambda i: (i, 0)),
                  pl.BlockSpec((1, d), lambda i: (0, 0)),
                  pl.BlockSpec((d, LANES), lambda i: (0, 0)),
                  pl.BlockSpec((1, LANES), lambda i: (0, 0))],
        out_specs=(pl.BlockSpec((tr, LANES), lambda i: (i, 0)),
                   pl.BlockSpec((1, LANES), lambda i: (0, 0)),
                   pl.BlockSpec((tr, d // 2), lambda i: (i, 0))),
        scratch_shapes=[pltpu.VMEM((1, LANES), F32)],
        compiler_params=_cparams(("arbitrary",)),
        name="router",
    )(h1, norm_w.reshape(1, d), wr, br)


def _gather_start(idx_ref, idx0, src_hbm, dst_ref, sem, n):
    def issue(r, c):
        pltpu.make_async_copy(src_hbm.at[pl.ds(idx_ref[idx0 + r], 1)], dst_ref.at[pl.ds(r, 1)],
                              sem).start()
        return c

    lax.fori_loop(0, n, issue, 0, unroll=8)


def _gather_wait(src_hbm, dst_ref, sem, n):
    pltpu.make_async_copy(src_hbm.at[pl.ds(0, n)], dst_ref.at[pl.ds(0, n)], sem).wait()


def _dispatch_kernel(ps_ref, pn_ref, used_ref, slot_hbm, xp_hbm, xs_hbm, slot_smem, zero_ref,
                     sem_i, sem_g, sem_z, *, top_k, tq, n_exp, n_tiles):
    i = pl.program_id(0)
    n = pl.num_programs(0)
    nrow = top_k * tq
    cp = pltpu.make_async_copy(slot_hbm.at[i], slot_smem, sem_i)
    cp.start()
    cp.wait()
    sem = sem_g.at[i % 2]

    def issue(r, c):
        src = xp_hbm.at[pl.ds(i * tq + r, 1)]
        for k in range(top_k):
            pltpu.make_async_copy(src, xs_hbm.at[pl.ds(slot_smem[0, k * tq + r], 1)],
                                  sem).start()
        return c

    lax.fori_loop(0, tq, issue, 0, unroll=2)

    def wait_tile(which):
        pltpu.make_async_copy(xp_hbm.at[pl.ds(0, nrow)], xs_hbm.at[pl.ds(0, nrow)],
                              sem_g.at[which]).wait()

    @pl.when(i > 0)
    def _():
        wait_tile((i + 1) % 2)

    @pl.when(i == n - 1)
    def _():
        wait_tile(i % 2)
        zero_ref[...] = jnp.zeros_like(zero_ref)
        zrow = zero_ref.at[pl.ds(0, 1)]

        def per_expert(e, c):
            start = ps_ref[e]
            cnt = pn_ref[e]

            def fill(r, c2):
                pltpu.make_async_copy(zrow, xs_hbm.at[pl.ds(start + r, 1)], sem_z).start()
                return c2

            def done(r, c2):
                pltpu.make_async_copy(zrow, xs_hbm.at[pl.ds(0, 1)], sem_z).wait()
                return c2

            lax.fori_loop(0, cnt, fill, 0)
            lax.fori_loop(0, cnt, done, 0)
            return c

        lax.fori_loop(0, n_exp, per_expert, 0)
        tm = zero_ref.shape[0]

        def fill_tile(j, c):
            cp_z = pltpu.make_async_copy(zero_ref, xs_hbm.at[pl.ds(j * tm, tm)], sem_z)
            cp_z.start()
            cp_z.wait()
            return c

        lax.fori_loop(used_ref[0], n_tiles, fill_tile, 0)


def _dispatch(slot_tiles, xp, pad_start, pad_n, tiles_used, *, n_tiles, tm, tq):
    t, half = xp.shape
    grid_spec = pltpu.PrefetchScalarGridSpec(
        num_scalar_prefetch=3,
        grid=(t // tq,),
        in_specs=[pl.BlockSpec(memory_space=pl.ANY), pl.BlockSpec(memory_space=pl.ANY)],
        out_specs=pl.BlockSpec(memory_space=pl.ANY),
        scratch_shapes=[pltpu.SMEM((1, TOP_K * tq), jnp.int32),
                        pltpu.VMEM((tm, half), jnp.uint32),
                        pltpu.SemaphoreType.DMA,
                        pltpu.SemaphoreType.DMA((2,)),
                        pltpu.SemaphoreType.DMA],
    )
    return pl.pallas_call(
        functools.partial(_dispatch_kernel, top_k=TOP_K, tq=tq, n_exp=pad_start.shape[0],
                          n_tiles=n_tiles),
        out_shape=jax.ShapeDtypeStruct((n_tiles * tm, half), jnp.uint32),
        grid_spec=grid_spec,
        compiler_params=_cparams(("arbitrary",)),
        name="dispatch",
    )(pad_start, pad_n, tiles_used, slot_tiles, xp)


def _moe_kernel(te_ref, ns_ref, xi_ref, xs_ref, wg_ref, wu_ref, bg_ref, bu_ref, wd_ref, bd_ref,
                o_ref, xn_ref, act_ref, *, nf, sub, nsub_max):
    del te_ref, xi_ref
    i = pl.program_id(0)
    s = pl.program_id(1)
    nsub = ns_ref[i]
    half = xs_ref.shape[1]
    tf = act_ref.shape[2]
    full = nsub == nsub_max
    part = (nsub > 0) & (nsub < nsub_max)

    def rows(c):
        return pl.ds(pl.multiple_of(c * sub, sub), sub)

    @pl.when((s == 0) & (nsub > 0))
    def _():
        def body(c, carry):
            lo, hi = _unpack_bf16_pairs(xs_ref[rows(c), :])
            xn_ref[rows(c), 0:half] = lo
            xn_ref[rows(c), half:2 * half] = hi
            return carry

        lax.fori_loop(0, nsub, body, 0)

    def swiglu(xn):
        hg = jnp.dot(xn, wg_ref[...], preferred_element_type=F32) + bg_ref[...]
        hu = jnp.dot(xn, wu_ref[...], preferred_element_type=F32) + bu_ref[...]
        g = jnp.minimum(hg, SWIGLU_LIMIT)
        u = jnp.clip(hu, -SWIGLU_LIMIT, SWIGLU_LIMIT)
        return ((u + 1.0) * (g * jax.nn.sigmoid(SWIGLU_ALPHA * g))).astype(BF16)

    def down(act_of):
        acc = jnp.dot(act_of(0), wd_ref[0:tf, :], preferred_element_type=F32)
        for f in range(1, nf):
            acc += jnp.dot(act_of(f), wd_ref[f * tf:(f + 1) * tf, :],
                           preferred_element_type=F32)
        return acc + bd_ref[...]

    sf = jnp.minimum(s, nf - 1)

    @pl.when((s < nf) & full)
    def _():
        act_ref[sf] = swiglu(xn_ref[...])

    @pl.when((s < nf) & part)
    def _():
        def body(c, carry):
            act_ref[sf, rows(c), :] = swiglu(xn_ref[rows(c), :])
            return carry

        lax.fori_loop(0, nsub, body, 0)

    @pl.when((s >= nf) & full)
    def _():
        o_ref[...] = down(lambda f: act_ref[f])

    @pl.when((s >= nf) & jnp.logical_not(full))
    def _():
        def body(c, carry):
            o_ref[rows(c), :] = down(lambda f: act_ref[f, rows(c), :])
            return carry

        lax.fori_loop(0, nsub, body, 0)

        def zero(c, carry):
            o_ref[rows(c), :] = jnp.zeros((sub, o_ref.shape[1]), o_ref.dtype)
            return carry

        lax.fori_loop(nsub, nsub_max, zero, 0)


def _moe(xs, tile_expert, tile_nsub, tile_x, w_gu, b_gu, w_dn, b_dn, *, sub, nsub_max, tf, tn):
    half = xs.shape[1]
    d = 2 * half
    n_exp, _, f2 = w_gu.shape
    fdim = f2 // 2
    nf = fdim // tf
    nn = d // tn
    tm = sub * nsub_max
    n_tiles = xs.shape[0] // tm
    last_f = nf - 1
    last_n = nn - 1

    def fi(i, s, te, ns, xi):
        return jnp.where(ns[i] != 0, jnp.minimum(s, last_f), last_f)

    def ni(i, s, te, ns, xi):
        return jnp.where(ns[i] != 0, jnp.maximum(s - nf, 0), last_n)

    def oi(i, s, te, ns, xi):
        return jnp.maximum(s - nf, 0)

    b_gu3 = b_gu.reshape(n_exp, 1, f2)
    b_dn3 = b_dn.reshape(n_exp, 1, d)
    grid_spec = pltpu.PrefetchScalarGridSpec(
        num_scalar_prefetch=3,
        grid=(n_tiles, nf + nn),
        in_specs=[pl.BlockSpec((tm, half), lambda i, s, te, ns, xi: (xi[i], 0),
                               pipeline_mode=pl.Buffered(1)),
                  pl.BlockSpec((None, d, tf), lambda i, s, te, ns, xi: (te[i], 0, fi(i, s, te, ns, xi))),
                  pl.BlockSpec((None, d, tf),
                               lambda i, s, te, ns, xi: (te[i], 0, nf + fi(i, s, te, ns, xi))),
                  pl.BlockSpec((None, 1, tf), lambda i, s, te, ns, xi: (te[i], 0, fi(i, s, te, ns, xi))),
                  pl.BlockSpec((None, 1, tf),
                               lambda i, s, te, ns, xi: (te[i], 0, nf + fi(i, s, te, ns, xi))),
                  pl.BlockSpec((None, fdim, tn), lambda i, s, te, ns, xi: (te[i], 0, ni(i, s, te, ns, xi))),
                  pl.BlockSpec((None, 1, tn), lambda i, s, te, ns, xi: (te[i], 0, ni(i, s, te, ns, xi)))],
        out_specs=pl.BlockSpec((tm, tn), lambda i, s, te, ns, xi: (i, oi(i, s, te, ns, xi))),
        scratch_shapes=[pltpu.VMEM((tm, d), BF16),
                        pltpu.VMEM((nf, tm, tf), BF16)],
    )
    return pl.pallas_call(
        functools.partial(_moe_kernel, nf=nf, sub=sub, nsub_max=nsub_max),
        out_shape=jax.ShapeDtypeStruct((n_tiles * tm, d), F32),
        grid_spec=grid_spec,
        compiler_params=_cparams(("arbitrary", "arbitrary"), vmem=MOE_VMEM_LIMIT),
        name="moe",
    )(tile_expert, tile_nsub, tile_x, xs, w_gu, w_gu, b_gu3, b_gu3, w_dn, b_dn3)


def _combine_kernel(slot_hbm, y_hbm, h_ref, g_ref, nw_ref, o_ref, slot_smem, buf, sem_i, sem_g,
                    *, top_k, tile0):
    i = pl.program_id(0)
    n = pl.num_programs(0)
    tq = h_ref.shape[0]
    nrow = top_k * tq

    def fetch(tile, b):
        cp = pltpu.make_async_copy(slot_hbm.at[tile], slot_smem.at[b], sem_i)
        cp.start()
        cp.wait()
        _gather_start(slot_smem.at[b].at[0], 0, y_hbm, buf.at[b], sem_g.at[b], nrow)

    def consume(b):
        _gather_wait(y_hbm, buf.at[b], sem_g.at[b], nrow)
        g = g_ref[...]
        h = h_ref[...]
        for k in range(top_k):
            h = h + g[:, k:k + 1] * buf[b, k * tq:(k + 1) * tq, :]
        ms = jnp.mean(h * h, axis=-1, keepdims=True)
        o_ref[...] = h * lax.rsqrt(ms + EPS) * nw_ref[...]

    @pl.when(i == 0)
    def _():
        fetch(tile0, 0)

    for b in range(2):
        @pl.when(i % 2 == b)
        def _():
            @pl.when(i + 1 < n)
            def _():
                fetch(tile0 + i + 1, 1 - b)

            consume(b)


def _combine(slot_tiles, y_sorted, h1, gates, final_w, *, row0, rows, tq, name):
    d = h1.shape[1]
    kern = functools.partial(_combine_kernel, top_k=TOP_K, tile0=row0 // tq)
    return pl.pallas_call(
        kern,
        out_shape=jax.ShapeDtypeStruct((rows, d), F32),
        grid=(rows // tq,),
        in_specs=[pl.BlockSpec(memory_space=pl.ANY),
                  pl.BlockSpec(memory_space=pl.ANY),
                  pl.BlockSpec((tq, d), lambda i: (row0 // tq + i, 0)),
                  pl.BlockSpec((tq, TOP_K), lambda i: (row0 // tq + i, 0)),
                  pl.BlockSpec((1, d), lambda i: (0, 0))],
        out_specs=pl.BlockSpec((tq, d), lambda i: (i, 0)),
        scratch_shapes=[pltpu.SMEM((2, 1, TOP_K * tq), jnp.int32),
                        pltpu.VMEM((2, TOP_K * tq, d), F32),
                        pltpu.SemaphoreType.DMA,
                        pltpu.SemaphoreType.DMA((2,))],
        compiler_params=_cparams(("arbitrary",)),
        name=name,
    )(slot_tiles, y_sorted, h1, gates, final_w.reshape(1, d))


def _rope_rows(pos):
    inv_freq = 1.0 / (ROPE_THETA ** (jnp.arange(0, HEAD_DIM, 2, dtype=F32) / HEAD_DIM))
    ang = pos.astype(F32)[:, None] * inv_freq[None, :]
    cos = jnp.cos(ang)
    sin = jnp.sin(ang)
    return (jnp.concatenate([cos, cos], axis=-1), jnp.concatenate([-sin, sin], axis=-1))


def _channel_dft(fw):
    gc = fw // N_FOURIER_GROUPS
    idx = jnp.arange(gc, dtype=jnp.int32)
    ang = ((idx[:, None] * idx[None, :]) % gc).astype(F32) * (2.0 * math.pi / gc)
    nrm = 1.0 / math.sqrt(gc)
    eye = jnp.eye(N_FOURIER_GROUPS, dtype=F32)
    return jnp.concatenate([jnp.kron(eye, jnp.cos(ang) * nrm),
                            jnp.kron(eye, -jnp.sin(ang) * nrm)], axis=1)


def _lambda_init(layer):
    return 0.8 - 0.6 * math.exp(-0.3 * layer)


def _mix_block(xs, meta_h, layer, p):
    (norm_mix_w, w_in, lq1, lk1, lq2, lk2, subln_w, w_ab, w_fb, w_out) = p
    d = w_in.shape[0]
    vw, fw = w_ab.shape[0], w_fb.shape[0]
    qkw = (w_in.shape[1] - vw - fw - 2 * d) // 2
    n_heads = vw // (2 * HEAD_DIM)
    u_off = 2 * qkw + vw
    gate_off = u_off + 2 * fw
    lam_init = _lambda_init(layer)
    rows = [x.shape[0] * x.shape[1] for x in xs]
    t = sum(rows)
    tm = _pick(math.gcd(*rows), 1024)

    w_u = w_in[:, u_off:u_off + fw]
    w_ucs = _matmul(w_u, _channel_dft(fw), F32, tm=_pick(d, 512), tn=_pick(2 * fw, 512),
                    precision=lax.Precision.HIGHEST, name="fold")
    tn_in = _pick(math.gcd(qkw, math.gcd(vw, math.gcd(2 * fw, d))), 512)
    w_ext = jnp.concatenate([w_in[:, :u_off].astype(BF16), w_ucs.astype(BF16),
                             w_in[:, u_off + fw:].astype(BF16)], axis=1)

    mt = meta_h.shape[0]
    cos_m, sin_m = _rope_rows(jnp.arange(mt, dtype=jnp.int32))
    proj_meta = _inproj(meta_h, norm_mix_w, w_ext, cos_m, sin_m, qkw=qkw, gate_off=gate_off,
                        tm=mt, tn=tn_in, name="inproj_meta")

    lam_vecs = [v.reshape(1, HEAD_DIM).astype(F32) for v in (lq1, lk1, lq2, lk2)]
    attn = jnp.zeros((t, vw), BF16)
    fmix = jnp.zeros((t, fw), BF16)
    projs, hs = [], []
    row0 = 0
    for gi, x in enumerate(xs):
        nb, s, _ = x.shape
        h = x.reshape(nb * s, d)
        cos, sin_s = _rope_rows(N_META + (jnp.arange(nb * s, dtype=jnp.int32) % s))
        proj = _inproj(h, norm_mix_w, w_ext, cos, sin_s, qkw=qkw, gate_off=gate_off, tm=tm,
                       tn=tn_in, name=f"inproj{gi}")
        attn = _attention(proj, proj_meta, lam_vecs, subln_w, attn, row0=row0, nb=nb, s=s,
                          n_heads=n_heads, qkw=qkw, lam_init=lam_init, tq=_pick(s, 512),
                          kc=_pick(s, 512), name=f"attn{gi}")
        fmix = _fourier(proj, proj_meta, _dft_tables(s), fmix, row0=row0, nb=nb, s=s, fw=fw,
                        uc_off=u_off, tm=_pick(s, 512), tn=_pick(fw, 512), name=f"fourier{gi}")
        projs.append(proj)
        hs.append(h)
        row0 += nb * s

    tn = _pick(math.gcd(d, gate_off), 512)
    merged = _merge(attn, fmix, w_ab.astype(BF16), w_fb.astype(BF16), projs, gate_off=gate_off,
                    tm=tm, tn=tn)
    return _outproj(merged, w_out.astype(BF16), hs, tm=tm, tn=tn)


def _moe_block(h1, rows, p, final_norm_w):
    t, d = h1.shape
    norm_moe_w, w_router, b_router, w_gu, b_gu, w_dn, b_dn = p
    n_exp = w_router.shape[1]
    fdim = w_dn.shape[1]
    sub = _pick(t, MOE_SUB)
    tm = sub * MOE_NSUB

    rout, cnt, xp = _router(h1, norm_moe_w, w_router, b_router, tr=_pick(t, 256))
    e_idx = rout[:, 0:TOP_K].astype(jnp.int32)
    gates = rout[:, TOP_K:2 * TOP_K]
    rank = rout[:, 2 * TOP_K:3 * TOP_K].astype(jnp.int32)
    counts = cnt[0, :n_exp].astype(jnp.int32)

    nsub_e = (counts + sub - 1) // sub
    tiles_e = (nsub_e + MOE_NSUB - 1) // MOE_NSUB
    tile_end = jnp.cumsum(tiles_e)
    tile_base = tile_end - tiles_e
    slot = tile_base[e_idx] * tm + rank
    n_tiles = -(-t * TOP_K // tm) + n_exp
    a_pad = n_tiles * tm
    pad_start = (tile_base * tm + counts).astype(jnp.int32)
    pad_n = (tiles_e * tm - counts).astype(jnp.int32)
    tile_id = jnp.arange(n_tiles, dtype=jnp.int32)
    tile_expert = jnp.minimum(jnp.searchsorted(tile_end, tile_id, side='right'),
                              n_exp - 1).astype(jnp.int32)
    tile_nsub = jnp.clip(nsub_e[tile_expert] - (tile_id - tile_base[tile_expert]) * MOE_NSUB,
                         0, MOE_NSUB)
    tile_nsub = jnp.where(tile_id < tile_end[-1], tile_nsub, 0).astype(jnp.int32)
    last_e = tile_expert[jnp.maximum(tile_end[-1] - 1, 0)]
    tile_expert = jnp.where(tile_nsub != 0, tile_expert, last_e)
    tile_x = jnp.where(tile_nsub != 0, tile_id, jnp.maximum(tile_end[-1] - 1, 0))

    tq = _pick(math.gcd(*rows), 128)
    slot_tiles = slot.reshape(t // tq, tq, TOP_K).transpose(0, 2, 1).reshape(t // tq, 1,
                                                                              TOP_K * tq)
    xs = _dispatch(slot_tiles, xp, pad_start, pad_n, tile_end[-1:].astype(jnp.int32),
                   n_tiles=n_tiles, tm=tm, tq=tq)
    y_sorted = _moe(xs, tile_expert, tile_nsub, tile_x.astype(jnp.int32), w_gu.astype(BF16),
                    b_gu, w_dn.astype(BF16), b_dn, sub=sub, nsub_max=MOE_NSUB,
                    tf=_pick(fdim, 512), tn=_pick(d, 256))
    outs = []
    row0 = 0
    for gi, r in enumerate(rows):
        outs.append(_combine(slot_tiles, y_sorted, h1, gates, final_norm_w, row0=row0, rows=r,
                             tq=tq, name=f"combine{gi}"))
        row0 += r
    return outs


def kernel(x_prompt, x_sample, meta_tokens, norm_mix_w, w_in, lambda_q1, lambda_k1, lambda_q2,
           lambda_k2, attn_subln_w, w_attn_branch, w_fourier_branch, w_out, norm_moe_w,
           w_router, b_router, w_gate_up, b_gate_up, w_down, b_down, final_norm_w):
    depth = norm_mix_w.shape[0]
    assert depth == 1, "meta rows are dropped after the first layer's mixing stage"
    d = x_prompt.shape[-1]
    xs = (x_prompt, x_sample)
    meta_h = jnp.zeros((LANES, d), F32).at[:N_META].set(meta_tokens.astype(F32))

    mix_p = (norm_mix_w[0], w_in[0], lambda_q1[0], lambda_k1[0], lambda_q2[0], lambda_k2[0],
             attn_subln_w[0], w_attn_branch[0], w_fourier_branch[0], w_out[0])
    h1 = _mix_block(xs, meta_h, 0, mix_p)
    moe_p = (norm_moe_w[0], w_router[0], b_router[0], w_gate_up[0], b_gate_up[0], w_down[0],
             b_down[0])
    rows = [x.shape[0] * x.shape[1] for x in xs]
    outs = _moe_block(h1, rows, moe_p, final_norm_w)
    return tuple(o.reshape(x.shape) for o, x in zip(outs, xs))
```

```python
import functools
import math

import jax
import jax.numpy as jnp
from jax import lax
from jax.experimental import pallas as pl
from jax.experimental.pallas import tpu as pltpu

F32 = jnp.float32
BF16 = jnp.bfloat16

N_META = 16
HEAD_DIM = 128
N_FOURIER_GROUPS = 4
ROPE_THETA = 10000.0
TOP_K = 4
SWIGLU_LIMIT = 7.0
SWIGLU_ALPHA = 1.702
EPS = 1e-6
LANES = 128
NEG_BIG = -1e30
LOG2E = math.log2(math.e)
VMEM_LIMIT = 56 * 1024 * 1024
MOE_VMEM_LIMIT = 60 * 1024 * 1024
MOE_SUB = 256
MOE_NSUB = 4


def _cparams(sem, vmem=VMEM_LIMIT, **kw):
    return pltpu.CompilerParams(dimension_semantics=sem, vmem_limit_bytes=vmem, **kw)


def _pick(n, pref):
    t = min(pref, n)
    while n % t:
        t //= 2
    return t


def _mm_kernel(a_ref, b_ref, o_ref, *, precision):
    o_ref[...] = jnp.dot(a_ref[...], b_ref[...], preferred_element_type=F32,
                         precision=precision).astype(o_ref.dtype)


def _matmul(a, b, out_dtype, *, tm, tn, precision=None, name):
    m, k = a.shape
    n = b.shape[1]
    return pl.pallas_call(
        functools.partial(_mm_kernel, precision=precision),
        out_shape=jax.ShapeDtypeStruct((m, n), out_dtype),
        grid=(m // tm, n // tn),
        in_specs=[pl.BlockSpec((tm, k), lambda i, j: (i, 0)),
                  pl.BlockSpec((k, tn), lambda i, j: (0, j))],
        out_specs=pl.BlockSpec((tm, tn), lambda i, j: (i, j)),
        compiler_params=_cparams(("parallel", "arbitrary")),
        name=name,
    )(a, b)


def _inproj_kernel(x_ref, nw_ref, w_ref, cos_ref, sin_ref, o_ref, xn_ref, *, nq, nk, jg, tn,
                   scale):
    j = pl.program_id(1)

    @pl.when(j == 0)
    def _():
        x = x_ref[...]
        ms = jnp.mean(x * x, axis=-1, keepdims=True)
        xn_ref[...] = (x * lax.rsqrt(ms + EPS) * nw_ref[...]).astype(BF16)

    acc = jnp.dot(xn_ref[...], w_ref[...], preferred_element_type=F32)

    def rope_store(s):
        cos = cos_ref[...]
        sin = sin_ref[...]
        for c in range(tn // HEAD_DIM):
            xc = acc[:, c * HEAD_DIM:(c + 1) * HEAD_DIM]
            rc = pltpu.roll(xc, HEAD_DIM // 2, 1)
            y = xc * cos + rc * sin
            if s != 1.0:
                y = y * s
            o_ref[:, c * HEAD_DIM:(c + 1) * HEAD_DIM] = y.astype(o_ref.dtype)

    @pl.when(j < nq)
    def _():
        rope_store(scale)

    @pl.when((j >= nq) & (j < nq + nk))
    def _():
        rope_store(1.0)

    @pl.when((j >= nq + nk) & (j < jg))
    def _():
        o_ref[...] = acc.astype(o_ref.dtype)

    @pl.when(j >= jg)
    def _():
        o_ref[...] = jax.nn.sigmoid(acc).astype(o_ref.dtype)


def _inproj(x, norm_w, w_ext, cos, sin_signed, *, qkw, gate_off, tm, tn, name):
    t, d = x.shape
    nj = w_ext.shape[1] // tn
    kern = functools.partial(_inproj_kernel, nq=qkw // tn, nk=qkw // tn, jg=gate_off // tn,
                             tn=tn, scale=HEAD_DIM ** -0.5 * LOG2E)
    return pl.pallas_call(
        kern,
        out_shape=jax.ShapeDtypeStruct((t, nj * tn), BF16),
        grid=(t // tm, nj),
        in_specs=[pl.BlockSpec((tm, d), lambda i, j: (i, 0), pipeline_mode=pl.Buffered(1)),
                  pl.BlockSpec((1, d), lambda i, j: (0, 0)),
                  pl.BlockSpec((d, tn), lambda i, j: (0, j)),
                  pl.BlockSpec((tm, HEAD_DIM), lambda i, j: (i, 0)),
                  pl.BlockSpec((tm, HEAD_DIM), lambda i, j: (i, 0))],
        out_specs=pl.BlockSpec((tm, tn), lambda i, j: (i, j)),
        scratch_shapes=[pltpu.VMEM((tm, d), BF16)],
        compiler_params=_cparams(("parallel", "arbitrary")),
        name=name,
    )(x, norm_w.reshape(1, d), w_ext, cos, sin_signed)


def _attn_kernel(q_ref, k_ref, v_ref, km_ref, vm_ref, lq1_ref, lk1_ref, lq2_ref, lk2_ref,
                 sw_ref, prev_ref, o_ref, *, lam_init, kc):
    del prev_ref
    d = HEAD_DIM
    lam = (jnp.exp(jnp.sum(lq1_ref[...] * lk1_ref[...], axis=-1, keepdims=True))
           - jnp.exp(jnp.sum(lq2_ref[...] * lk2_ref[...], axis=-1, keepdims=True))
           + lam_init)
    nt = (((1,), (1,)), ((), ()))
    tq = q_ref.shape[0]
    s_len = k_ref.shape[0]
    mrows = km_ref.shape[0]
    meta_bias = jnp.where(lax.broadcasted_iota(jnp.int32, (tq, mrows), 1) < N_META, 0.0, NEG_BIG)
    chunks = [(km_ref, vm_ref, 0, mrows, True)]
    chunks += [(k_ref, v_ref, c * kc, kc, False) for c in range(s_len // kc)]

    qs = [q_ref[:, m * d:(m + 1) * d] for m in range(2)]
    mx = [None, None]
    l = [None, None]
    acc = [None, None]
    for (kr, vr, off, n, is_meta) in chunks:
        v = vr[off:off + n, :]
        for m in range(2):
            s = lax.dot_general(qs[m], kr[off:off + n, m * d:(m + 1) * d], nt,
                                preferred_element_type=F32)
            if is_meta:
                s = s + meta_bias
            cmx = jnp.max(s, axis=-1, keepdims=True)
            if mx[m] is None:
                mx[m] = cmx
                p = jnp.exp2(s - cmx)
                l[m] = jnp.sum(p, axis=-1, keepdims=True)
                acc[m] = jnp.dot(p.astype(BF16), v, preferred_element_type=F32)
            else:
                mn = jnp.maximum(mx[m], cmx)
                alpha = jnp.exp2(mx[m] - mn)
                p = jnp.exp2(s - mn)
                l[m] = alpha * l[m] + jnp.sum(p, axis=-1, keepdims=True)
                acc[m] = alpha * acc[m] + jnp.dot(p.astype(BF16), v,
                                                  preferred_element_type=F32)
                mx[m] = mn

    o = acc[0] * (1.0 / l[0]) - acc[1] * (lam / l[1])
    ms = jnp.mean(o * o, axis=-1, keepdims=True)
    o = o * lax.rsqrt(ms + EPS) * sw_ref[...] * (1.0 - lam_init)
    o_ref[...] = o.astype(o_ref.dtype)


def _attention(proj, proj_meta, lam_vecs, subln_w, prev, *, row0, nb, s, n_heads, qkw, lam_init,
               tq, kc, name):
    hw = 2 * HEAD_DIM
    kcol = qkw // hw
    vcol = 2 * qkw // hw
    ob = row0 // tq
    mrows = proj_meta.shape[0]
    vec = lambda: pl.BlockSpec((1, HEAD_DIM), lambda b, h, i: (0, 0))
    in_specs = [pl.BlockSpec((tq, hw), lambda b, h, i: (b * (s // tq) + i, h)),
                pl.BlockSpec((s, hw), lambda b, h, i: (b, kcol + h)),
                pl.BlockSpec((s, hw), lambda b, h, i: (b, vcol + h)),
                pl.BlockSpec((mrows, hw), lambda b, h, i: (0, kcol + h)),
                pl.BlockSpec((mrows, hw), lambda b, h, i: (0, vcol + h)),
                vec(), vec(), vec(), vec(),
                pl.BlockSpec((1, hw), lambda b, h, i: (0, 0)),
                pl.BlockSpec(memory_space=pl.ANY)]
    args = [proj, proj, proj, proj_meta, proj_meta, *lam_vecs, subln_w.reshape(1, hw), prev]
    return pl.pallas_call(
        functools.partial(_attn_kernel, lam_init=lam_init, kc=kc),
        out_shape=jax.ShapeDtypeStruct(prev.shape, prev.dtype),
        grid=(nb, n_heads, s // tq),
        in_specs=in_specs,
        out_specs=pl.BlockSpec((tq, hw), lambda b, h, i: (ob + b * (s // tq) + i, h)),
        input_output_aliases={len(args) - 1: 0},
        compiler_params=_cparams(("parallel", "parallel", "arbitrary")),
        name=name,
    )(*args)


def _fourier_kernel(ac_ref, as_ref, uc_ref, us_ref, mc_ref, msn_ref, ucm_ref, usm_ref, prev_ref,
                    o_ref):
    del prev_ref
    acc = jnp.dot(ac_ref[...], uc_ref[...], preferred_element_type=F32)
    acc += jnp.dot(as_ref[...], us_ref[...], preferred_element_type=F32)
    acc += jnp.dot(mc_ref[...], ucm_ref[...], preferred_element_type=F32)
    acc += jnp.dot(msn_ref[...], usm_ref[...], preferred_element_type=F32)
    o_ref[...] = acc.astype(o_ref.dtype)


def _fourier(proj, proj_meta, tabs, prev, *, row0, nb, s, fw, uc_off, tm, tn, name):
    a_c, a_s, m_c, m_s = tabs
    mrows = proj_meta.shape[0]
    ob = row0 // tm
    ucb = uc_off // tn
    usb = (uc_off + fw) // tn
    in_specs = [pl.BlockSpec((tm, s), lambda b, i, j: (i, 0)),
                pl.BlockSpec((tm, s), lambda b, i, j: (i, 0)),
                pl.BlockSpec((s, tn), lambda b, i, j: (b, ucb + j)),
                pl.BlockSpec((s, tn), lambda b, i, j: (b, usb + j)),
                pl.BlockSpec((tm, mrows), lambda b, i, j: (i, 0)),
                pl.BlockSpec((tm, mrows), lambda b, i, j: (i, 0)),
                pl.BlockSpec((mrows, tn), lambda b, i, j: (0, ucb + j)),
                pl.BlockSpec((mrows, tn), lambda b, i, j: (0, usb + j)),
                pl.BlockSpec(memory_space=pl.ANY)]
    args = [a_c, a_s, proj, proj, m_c, m_s, proj_meta, proj_meta, prev]
    return pl.pallas_call(
        _fourier_kernel,
        out_shape=jax.ShapeDtypeStruct(prev.shape, prev.dtype),
        grid=(nb, s // tm, fw // tn),
        in_specs=in_specs,
        out_specs=pl.BlockSpec((tm, tn), lambda b, i, j: (ob + b * (s // tm) + i, j)),
        input_output_aliases={len(args) - 1: 0},
        compiler_params=_cparams(("parallel", "parallel", "arbitrary")),
        name=name,
    )(*args)


def _dft_tables(s):
    length = s + N_META
    pos_r = jnp.arange(s, dtype=jnp.int32) + N_META
    pos_m = jnp.where(jnp.arange(LANES) < N_META, jnp.arange(LANES, dtype=jnp.int32), 0)
    w = 2.0 * math.pi / length
    nrm = 1.0 / math.sqrt(length)

    def tab(pc, mask=None):
        ang = ((pos_r[:, None] * pc[None, :]) % length).astype(F32) * w
        c = jnp.cos(ang) * nrm
        sn = jnp.sin(ang) * nrm
        if mask is not None:
            c = jnp.where(mask[None, :], c, 0.0)
            sn = jnp.where(mask[None, :], sn, 0.0)
        return c.astype(BF16), sn.astype(BF16)

    a_c, a_s = tab(pos_r)
    m_c, m_s = tab(pos_m, jnp.arange(LANES) < N_META)
    return a_c, a_s, m_c, m_s


def _two_source(n0, nj):
    first = lambda col: (lambda i, j: (jnp.minimum(i, n0 - 1),
                                       col(jnp.where(i < n0, j, nj - 1))))
    second = lambda col: (lambda i, j: (jnp.maximum(i - n0, 0), col(jnp.where(i < n0, 0, j))))
    return first, second


def _merge_kernel(a_ref, f_ref, wa_ref, wf_ref, g0a_ref, g1a_ref, g0b_ref, g1b_ref, o_ref, *,
                  n0):
    ab = jnp.dot(a_ref[...], wa_ref[...], preferred_element_type=F32)
    fb = jnp.dot(f_ref[...], wf_ref[...], preferred_element_type=F32)
    i = pl.program_id(0)

    @pl.when(i < n0)
    def _():
        m = g0a_ref[...].astype(F32) * ab + g1a_ref[...].astype(F32) * fb
        o_ref[...] = m.astype(o_ref.dtype)

    @pl.when(i >= n0)
    def _():
        m = g0b_ref[...].astype(F32) * ab + g1b_ref[...].astype(F32) * fb
        o_ref[...] = m.astype(o_ref.dtype)


def _merge(attn, fmix, w_ab, w_fb, projs, *, gate_off, tm, tn):
    t, vw = attn.shape
    fw = fmix.shape[1]
    d = w_ab.shape[1]
    nj = d // tn
    n0 = projs[0].shape[0] // tm
    g0b = gate_off // tn
    g1b = (gate_off + d) // tn
    first, second = _two_source(n0, nj)
    return pl.pallas_call(
        functools.partial(_merge_kernel, n0=n0),
        out_shape=jax.ShapeDtypeStruct((t, d), BF16),
        grid=(t // tm, nj),
        in_specs=[pl.BlockSpec((tm, vw), lambda i, j: (i, 0)),
                  pl.BlockSpec((tm, fw), lambda i, j: (i, 0)),
                  pl.BlockSpec((vw, tn), lambda i, j: (0, j)),
                  pl.BlockSpec((fw, tn), lambda i, j: (0, j)),
                  pl.BlockSpec((tm, tn), first(lambda j: g0b + j)),
                  pl.BlockSpec((tm, tn), first(lambda j: g1b + j)),
                  pl.BlockSpec((tm, tn), second(lambda j: g0b + j)),
                  pl.BlockSpec((tm, tn), second(lambda j: g1b + j))],
        out_specs=pl.BlockSpec((tm, tn), lambda i, j: (i, j)),
        compiler_params=_cparams(("parallel", "arbitrary")),
        name="merge",
    )(attn, fmix, w_ab, w_fb, projs[0], projs[0], projs[1], projs[1])


def _outproj_kernel(m_ref, w_ref, ha_ref, hb_ref, o_ref, *, n0):
    acc = jnp.dot(m_ref[...], w_ref[...], preferred_element_type=F32)
    i = pl.program_id(0)

    @pl.when(i < n0)
    def _():
        o_ref[...] = ha_ref[...] + acc

    @pl.when(i >= n0)
    def _():
        o_ref[...] = hb_ref[...] + acc


def _outproj(merged, w_out, hs, *, tm, tn):
    t, d = merged.shape
    nj = d // tn
    n0 = hs[0].shape[0] // tm
    first, second = _two_source(n0, nj)
    return pl.pallas_call(
        functools.partial(_outproj_kernel, n0=n0),
        out_shape=jax.ShapeDtypeStruct((t, d), F32),
        grid=(t // tm, nj),
        in_specs=[pl.BlockSpec((tm, d), lambda i, j: (i, 0)),
                  pl.BlockSpec((d, tn), lambda i, j: (0, j)),
                  pl.BlockSpec((tm, tn), first(lambda j: j)),
                  pl.BlockSpec((tm, tn), second(lambda j: j))],
        out_specs=pl.BlockSpec((tm, tn), lambda i, j: (i, j)),
        compiler_params=_cparams(("parallel", "arbitrary")),
        name="outproj",
    )(merged, w_out, hs[0], hs[1])


def _pack_bf16_pairs(x):
    n = x.shape[1] // 2
    bits = lax.bitcast_convert_type(x.astype(BF16).astype(F32), jnp.uint32)
    return (bits[:, :n] >> 16) | (bits[:, n:] & jnp.uint32(0xFFFF0000))


def _unpack_bf16_pairs(w):
    lo = lax.bitcast_convert_type(w << 16, F32)
    hi = lax.bitcast_convert_type(w & jnp.uint32(0xFFFF0000), F32)
    return lo.astype(BF16), hi.astype(BF16)


def _router_kernel(h_ref, nw_ref, wr_ref, br_ref, o_ref, cnt_ref, xp_ref, carry_ref, *, top_k):
    i = pl.program_id(0)

    @pl.when(i == 0)
    def _():
        carry_ref[...] = jnp.zeros_like(carry_ref)

    x = h_ref[...]
    tr = x.shape[0]
    ms = jnp.mean(x * x, axis=-1, keepdims=True)
    xn = x * lax.rsqrt(ms + EPS) * nw_ref[...]
    xp_ref[...] = _pack_bf16_pairs(xn)
    logits = jnp.dot(xn, wr_ref[...], preferred_element_type=F32,
                     precision=lax.Precision.HIGHEST) + br_ref[...]
    lane = lax.broadcasted_iota(jnp.int32, logits.shape, 1)
    work = logits
    idxs, vals = [], []
    sel = jnp.zeros(logits.shape, F32)
    for _ in range(top_k):
        mx = jnp.max(work, axis=-1, keepdims=True)
        ix = jnp.min(jnp.where(work == mx, lane, LANES), axis=-1, keepdims=True)
        hit = lane == ix
        sel = jnp.where(hit, 1.0, sel)
        work = jnp.where(hit, -jnp.inf, work)
        idxs.append(ix)
        vals.append(mx)
    ex = [jnp.exp(v - vals[0]) for v in vals]
    den = ex[0]
    for e in ex[1:]:
        den = den + e
    inv = 1.0 / den
    r_i = lax.broadcasted_iota(jnp.int32, (tr, tr), 0)
    c_i = lax.broadcasted_iota(jnp.int32, (tr, tr), 1)
    tril = jnp.where(c_i < r_i, 1.0, 0.0).astype(BF16)
    rank_dense = jnp.dot(tril, sel.astype(BF16), preferred_element_type=F32) + carry_ref[...]
    out = jnp.zeros(logits.shape, F32)
    for k in range(top_k):
        rk = jnp.sum(jnp.where(lane == idxs[k], rank_dense, 0.0), axis=-1, keepdims=True)
        out = jnp.where(lane == k, idxs[k].astype(F32), out)
        out = jnp.where(lane == top_k + k, ex[k] * inv, out)
        out = jnp.where(lane == 2 * top_k + k, rk, out)
    o_ref[...] = out
    carry_ref[...] += jnp.sum(sel, axis=0, keepdims=True)
    cnt_ref[...] = carry_ref[...]


def _router(h1, norm_w, w_router, b_router, *, tr):
    t, d = h1.shape
    e = w_router.shape[1]
    wr = jnp.zeros((d, LANES), F32).at[:, :e].set(w_router)
    br = jnp.full((1, LANES), NEG_BIG, F32).at[0, :e].set(b_router)
    return pl.pallas_call(
        functools.partial(_router_kernel, top_k=TOP_K),
        out_shape=(jax.ShapeDtypeStruct((t, LANES), F32),
                   jax.ShapeDtypeStruct((1, LANES), F32),
                   jax.ShapeDtypeStruct((t, d // 2), jnp.uint32)),
        grid=(t // tr,),
        in_specs=[pl.BlockSpec((tr, d), lambda i: (i, 0)),
                  pl.BlockSpec((1, d), lambda i: (0, 0)),
                  pl.BlockSpec((d, LANES), lambda i: (0, 0)),
                  pl.BlockSpec((1, LANES), lambda i: (0, 0))],
        out_specs=(pl.BlockSpec((tr, LANES), lambda i: (i, 0)),
                   pl.BlockSpec((1, LANES), lambda i: (0, 0)),
                   pl.BlockSpec((tr, d // 2), lambda i: (i, 0))),
        scratch_shapes=[pltpu.VMEM((1, LANES), F32)],
        compiler_params=_cparams(("arbitrary",)),
        name="router",
    )(h1, norm_w.reshape(1, d), wr, br)


def _gather_start(idx_ref, idx0, src_hbm, dst_ref, sem, n):
    def issue(r, c):
        pltpu.make_async_copy(src_hbm.at[pl.ds(idx_ref[idx0 + r], 1)], dst_ref.at[pl.ds(r, 1)],
                              sem).start()
        return c

    lax.fori_loop(0, n, issue, 0, unroll=8)


def _gather_wait(src_hbm, dst_ref, sem, n):
    pltpu.make_async_copy(src_hbm.at[pl.ds(0, n)], dst_ref.at[pl.ds(0, n)], sem).wait()


def _dispatch_kernel(ps_ref, pn_ref, used_ref, slot_hbm, xp_ref, xs_hbm, slot_smem, zero_ref,
                     sem_i, sem_g, sem_z, *, top_k, tq, n_exp, n_tiles):
    i = pl.program_id(0)
    n = pl.num_programs(0)
    cp = pltpu.make_async_copy(slot_hbm.at[i], slot_smem, sem_i)
    cp.start()
    cp.wait()

    def issue(r, c):
        src = xp_ref.at[pl.ds(r, 1)]
        for k in range(top_k):
            pltpu.make_async_copy(src, xs_hbm.at[pl.ds(slot_smem[0, k * tq + r], 1)],
                                  sem_g).start()
        return c

    lax.fori_loop(0, tq, issue, 0, unroll=2)
    for k in range(top_k):
        pltpu.make_async_copy(xp_ref, xs_hbm.at[pl.ds(0, tq)], sem_g).wait()

    @pl.when(i == n - 1)
    def _():
        zero_ref[...] = jnp.zeros_like(zero_ref)
        zrow = zero_ref.at[pl.ds(0, 1)]

        def per_expert(e, c):
            start = ps_ref[e]
            cnt = pn_ref[e]

            def fill(r, c2):
                pltpu.make_async_copy(zrow, xs_hbm.at[pl.ds(start + r, 1)], sem_z).start()
                return c2

            def done(r, c2):
                pltpu.make_async_copy(zrow, xs_hbm.at[pl.ds(0, 1)], sem_z).wait()
                return c2

            lax.fori_loop(0, cnt, fill, 0)
            lax.fori_loop(0, cnt, done, 0)
            return c

        lax.fori_loop(0, n_exp, per_expert, 0)
        tm = zero_ref.shape[0]

        def fill_tile(j, c):
            cp_z = pltpu.make_async_copy(zero_ref, xs_hbm.at[pl.ds(j * tm, tm)], sem_z)
            cp_z.start()
            cp_z.wait()
            return c

        lax.fori_loop(used_ref[0], n_tiles, fill_tile, 0)


def _dispatch(slot_tiles, xp, pad_start, pad_n, tiles_used, *, n_tiles, tm, tq):
    t, half = xp.shape
    grid_spec = pltpu.PrefetchScalarGridSpec(
        num_scalar_prefetch=3,
        grid=(t // tq,),
        in_specs=[pl.BlockSpec(memory_space=pl.ANY),
                  pl.BlockSpec((tq, half), lambda i, ps, pn, used: (i, 0))],
        out_specs=pl.BlockSpec(memory_space=pl.ANY),
        scratch_shapes=[pltpu.SMEM((1, TOP_K * tq), jnp.int32),
                        pltpu.VMEM((tm, half), jnp.uint32),
                        pltpu.SemaphoreType.DMA,
                        pltpu.SemaphoreType.DMA,
                        pltpu.SemaphoreType.DMA],
    )
    return pl.pallas_call(
        functools.partial(_dispatch_kernel, top_k=TOP_K, tq=tq, n_exp=pad_start.shape[0],
                          n_tiles=n_tiles),
        out_shape=jax.ShapeDtypeStruct((n_tiles * tm, half), jnp.uint32),
        grid_spec=grid_spec,
        compiler_params=_cparams(("arbitrary",)),
        name="dispatch",
    )(pad_start, pad_n, tiles_used, slot_tiles, xp)


def _moe_kernel(te_ref, ns_ref, xi_ref, xs_ref, wg_ref, wu_ref, bg_ref, bu_ref, wd_ref, bd_ref,
                o_ref, xn_ref, act_ref, *, nf, sub, nsub_max):
    del te_ref, xi_ref
    i = pl.program_id(0)
    s = pl.program_id(1)
    nsub = ns_ref[i]
    half = xs_ref.shape[1]
    tf = act_ref.shape[2]
    full = nsub == nsub_max
    part = (nsub > 0) & (nsub < nsub_max)

    def rows(c):
        return pl.ds(pl.multiple_of(c * sub, sub), sub)

    @pl.when((s == 0) & (nsub > 0))
    def _():
        def body(c, carry):
            lo, hi = _unpack_bf16_pairs(xs_ref[rows(c), :])
            xn_ref[rows(c), 0:half] = lo
            xn_ref[rows(c), half:2 * half] = hi
            return carry

        lax.fori_loop(0, nsub, body, 0)

    def swiglu(xn):
        hg = jnp.dot(xn, wg_ref[...], preferred_element_type=F32) + bg_ref[...]
        hu = jnp.dot(xn, wu_ref[...], preferred_element_type=F32) + bu_ref[...]
        g = jnp.minimum(hg, SWIGLU_LIMIT)
        u = jnp.clip(hu, -SWIGLU_LIMIT, SWIGLU_LIMIT)
        return ((u + 1.0) * (g * jax.nn.sigmoid(SWIGLU_ALPHA * g))).astype(BF16)

    def down(act_of):
        acc = jnp.dot(act_of(0), wd_ref[0:tf, :], preferred_element_type=F32)
        for f in range(1, nf):
            acc += jnp.dot(act_of(f), wd_ref[f * tf:(f + 1) * tf, :],
                           preferred_element_type=F32)
        return acc + bd_ref[...]

    sf = jnp.minimum(s, nf - 1)

    @pl.when((s < nf) & full)
    def _():
        act_ref[sf] = swiglu(xn_ref[...])

    @pl.when((s < nf) & part)
    def _():
        def body(c, carry):
            act_ref[sf, rows(c), :] = swiglu(xn_ref[rows(c), :])
            return carry

        lax.fori_loop(0, nsub, body, 0)

    @pl.when((s >= nf) & full)
    def _():
        o_ref[...] = down(lambda f: act_ref[f])

    @pl.when((s >= nf) & jnp.logical_not(full))
    def _():
        def body(c, carry):
            o_ref[rows(c), :] = down(lambda f: act_ref[f, rows(c), :])
            return carry

        lax.fori_loop(0, nsub, body, 0)

        def zero(c, carry):
            o_ref[rows(c), :] = jnp.zeros((sub, o_ref.shape[1]), o_ref.dtype)
            return carry

        lax.fori_loop(nsub, nsub_max, zero, 0)


def _moe(xs, tile_expert, tile_nsub, tile_x, w_gu, b_gu, w_dn, b_dn, *, sub, nsub_max, tf, tn):
    half = xs.shape[1]
    d = 2 * half
    n_exp, _, f2 = w_gu.shape
    fdim = f2 // 2
    nf = fdim // tf
    nn = d // tn
    tm = sub * nsub_max
    n_tiles = xs.shape[0] // tm
    last_n = nn - 1

    def ge(i, s, te, ns, xi):
        nxt = jnp.minimum(i + 1, n_tiles - 1)
        return jnp.where((ns[i] != 0) & (s >= nf), te[nxt], te[i])

    def fi(i, s, te, ns, xi):
        return jnp.where((ns[i] != 0) & (s < nf), s, 0)

    def ni(i, s, te, ns, xi):
        return jnp.where(ns[i] != 0, jnp.maximum(s - nf, 0), last_n)

    def oi(i, s, te, ns, xi):
        return jnp.maximum(s - nf, 0)

    b_gu3 = b_gu.reshape(n_exp, 1, f2)
    b_dn3 = b_dn.reshape(n_exp, 1, d)
    grid_spec = pltpu.PrefetchScalarGridSpec(
        num_scalar_prefetch=3,
        grid=(n_tiles, nf + nn),
        in_specs=[pl.BlockSpec((tm, half), lambda i, s, te, ns, xi: (xi[i], 0),
                               pipeline_mode=pl.Buffered(1)),
                  pl.BlockSpec((None, d, tf),
                               lambda *a: (ge(*a), 0, fi(*a))),
                  pl.BlockSpec((None, d, tf),
                               lambda *a: (ge(*a), 0, nf + fi(*a))),
                  pl.BlockSpec((None, 1, tf),
                               lambda *a: (ge(*a), 0, fi(*a))),
                  pl.BlockSpec((None, 1, tf),
                               lambda *a: (ge(*a), 0, nf + fi(*a))),
                  pl.BlockSpec((None, fdim, tn), lambda i, s, te, ns, xi: (te[i], 0, ni(i, s, te, ns, xi))),
                  pl.BlockSpec((None, 1, tn), lambda i, s, te, ns, xi: (te[i], 0, ni(i, s, te, ns, xi)))],
        out_specs=pl.BlockSpec((tm, tn), lambda i, s, te, ns, xi: (i, oi(i, s, te, ns, xi))),
        scratch_shapes=[pltpu.VMEM((tm, d), BF16),
                        pltpu.VMEM((nf, tm, tf), BF16)],
    )
    return pl.pallas_call(
        functools.partial(_moe_kernel, nf=nf, sub=sub, nsub_max=nsub_max),
        out_shape=jax.ShapeDtypeStruct((n_tiles * tm, d), F32),
        grid_spec=grid_spec,
        compiler_params=_cparams(("arbitrary", "arbitrary"), vmem=MOE_VMEM_LIMIT),
        name="moe",
    )(tile_expert, tile_nsub, tile_x, xs, w_gu, w_gu, b_gu3, b_gu3, w_dn, b_dn3)


def _combine_kernel(slot_hbm, y_hbm, h_ref, g_ref, nw_ref, o_ref, slot_smem, buf, sem_i, sem_g,
                    *, top_k, tile0):
    i = pl.program_id(0)
    n = pl.num_programs(0)
    tq = h_ref.shape[0]
    nrow = top_k * tq

    def fetch(tile, b):
        cp = pltpu.make_async_copy(slot_hbm.at[tile], slot_smem.at[b], sem_i)
        cp.start()
        cp.wait()
        _gather_start(slot_smem.at[b].at[0], 0, y_hbm, buf.at[b], sem_g.at[b], nrow)

    def consume(b):
        _gather_wait(y_hbm, buf.at[b], sem_g.at[b], nrow)
        g = g_ref[...]
        h = h_ref[...]
        for k in range(top_k):
            h = h + g[:, k:k + 1] * buf[b, k * tq:(k + 1) * tq, :]
        ms = jnp.mean(h * h, axis=-1, keepdims=True)
        o_ref[...] = h * lax.rsqrt(ms + EPS) * nw_ref[...]

    @pl.when(i == 0)
    def _():
        fetch(tile0, 0)

    for b in range(2):
        @pl.when(i % 2 == b)
        def _():
            @pl.when(i + 1 < n)
            def _():
                fetch(tile0 + i + 1, 1 - b)

            consume(b)


def _combine(slot_tiles, y_sorted, h1, gates, final_w, *, row0, rows, tq, name):
    d = h1.shape[1]
    kern = functools.partial(_combine_kernel, top_k=TOP_K, tile0=row0 // tq)
    return pl.pallas_call(
        kern,
        out_shape=jax.ShapeDtypeStruct((rows, d), F32),
        grid=(rows // tq,),
        in_specs=[pl.BlockSpec(memory_space=pl.ANY),
                  pl.BlockSpec(memory_space=pl.ANY),
                  pl.BlockSpec((tq, d), lambda i: (row0 // tq + i, 0)),
                  pl.BlockSpec((tq, TOP_K), lambda i: (row0 // tq + i, 0)),
                  pl.BlockSpec((1, d), lambda i: (0, 0))],
        out_specs=pl.BlockSpec((tq, d), lambda i: (i, 0)),
        scratch_shapes=[pltpu.SMEM((2, 1, TOP_K * tq), jnp.int32),
                        pltpu.VMEM((2, TOP_K * tq, d), F32),
                        pltpu.SemaphoreType.DMA,
                        pltpu.SemaphoreType.DMA((2,))],
        compiler_params=_cparams(("arbitrary",)),
        name=name,
    )(slot_tiles, y_sorted, h1, gates, final_w.reshape(1, d))


def _rope_rows(pos):
    inv_freq = 1.0 / (ROPE_THETA ** (jnp.arange(0, HEAD_DIM, 2, dtype=F32) / HEAD_DIM))
    ang = pos.astype(F32)[:, None] * inv_freq[None, :]
    cos = jnp.cos(ang)
    sin = jnp.sin(ang)
    return (jnp.concatenate([cos, cos], axis=-1), jnp.concatenate([-sin, sin], axis=-1))


def _channel_dft(fw):
    gc = fw // N_FOURIER_GROUPS
    idx = jnp.arange(gc, dtype=jnp.int32)
    ang = ((idx[:, None] * idx[None, :]) % gc).astype(F32) * (2.0 * math.pi / gc)
    nrm = 1.0 / math.sqrt(gc)
    eye = jnp.eye(N_FOURIER_GROUPS, dtype=F32)
    return jnp.concatenate([jnp.kron(eye, jnp.cos(ang) * nrm),
                            jnp.kron(eye, -jnp.sin(ang) * nrm)], axis=1)


def _lambda_init(layer):
    return 0.8 - 0.6 * math.exp(-0.3 * layer)


def _mix_block(xs, meta_h, layer, p):
    (norm_mix_w, w_in, lq1, lk1, lq2, lk2, subln_w, w_ab, w_fb, w_out) = p
    d = w_in.shape[0]
    vw, fw = w_ab.shape[0], w_fb.shape[0]
    qkw = (w_in.shape[1] - vw - fw - 2 * d) // 2
    n_heads = vw // (2 * HEAD_DIM)
    u_off = 2 * qkw + vw
    gate_off = u_off + 2 * fw
    lam_init = _lambda_init(layer)
    rows = [x.shape[0] * x.shape[1] for x in xs]
    t = sum(rows)
    tm = _pick(math.gcd(*rows), 1024)

    w_u = w_in[:, u_off:u_off + fw]
    w_ucs = _matmul(w_u, _channel_dft(fw), F32, tm=_pick(d, 512), tn=_pick(2 * fw, 512),
                    precision=lax.Precision.HIGHEST, name="fold")
    tn_in = _pick(math.gcd(qkw, math.gcd(vw, math.gcd(2 * fw, d))), 512)
    w_ext = jnp.concatenate([w_in[:, :u_off].astype(BF16), w_ucs.astype(BF16),
                             w_in[:, u_off + fw:].astype(BF16)], axis=1)

    mt = meta_h.shape[0]
    cos_m, sin_m = _rope_rows(jnp.arange(mt, dtype=jnp.int32))
    proj_meta = _inproj(meta_h, norm_mix_w, w_ext, cos_m, sin_m, qkw=qkw, gate_off=gate_off,
                        tm=mt, tn=tn_in, name="inproj_meta")

    lam_vecs = [v.reshape(1, HEAD_DIM).astype(F32) for v in (lq1, lk1, lq2, lk2)]
    attn = jnp.zeros((t, vw), BF16)
    fmix = jnp.zeros((t, fw), BF16)
    projs, hs = [], []
    row0 = 0
    for gi, x in enumerate(xs):
        nb, s, _ = x.shape
        h = x.reshape(nb * s, d)
        cos, sin_s = _rope_rows(N_META + (jnp.arange(nb * s, dtype=jnp.int32) % s))
        proj = _inproj(h, norm_mix_w, w_ext, cos, sin_s, qkw=qkw, gate_off=gate_off, tm=tm,
                       tn=tn_in, name=f"inproj{gi}")
        attn = _attention(proj, proj_meta, lam_vecs, subln_w, attn, row0=row0, nb=nb, s=s,
                          n_heads=n_heads, qkw=qkw, lam_init=lam_init, tq=_pick(s, 512),
                          kc=_pick(s, 512), name=f"attn{gi}")
        fmix = _fourier(proj, proj_meta, _dft_tables(s), fmix, row0=row0, nb=nb, s=s, fw=fw,
                        uc_off=u_off, tm=_pick(s, 512), tn=_pick(fw, 512), name=f"fourier{gi}")
        projs.append(proj)
        hs.append(h)
        row0 += nb * s

    tn = _pick(math.gcd(d, gate_off), 512)
    merged = _merge(attn, fmix, w_ab.astype(BF16), w_fb.astype(BF16), projs, gate_off=gate_off,
                    tm=tm, tn=tn)
    return _outproj(merged, w_out.astype(BF16), hs, tm=tm, tn=tn)


def _moe_block(h1, rows, p, final_norm_w):
    t, d = h1.shape
    norm_moe_w, w_router, b_router, w_gu, b_gu, w_dn, b_dn = p
    n_exp = w_router.shape[1]
    fdim = w_dn.shape[1]
    sub = _pick(t, MOE_SUB)
    tm = sub * MOE_NSUB

    rout, cnt, xp = _router(h1, norm_moe_w, w_router, b_router, tr=_pick(t, 256))
    e_idx = rout[:, 0:TOP_K].astype(jnp.int32)
    gates = rout[:, TOP_K:2 * TOP_K]
    rank = rout[:, 2 * TOP_K:3 * TOP_K].astype(jnp.int32)
    counts = cnt[0, :n_exp].astype(jnp.int32)

    nsub_e = (counts + sub - 1) // sub
    tiles_e = (nsub_e + MOE_NSUB - 1) // MOE_NSUB
    tile_end = jnp.cumsum(tiles_e)
    tile_base = tile_end - tiles_e
    slot = tile_base[e_idx] * tm + rank
    n_tiles = -(-t * TOP_K // tm) + n_exp
    a_pad = n_tiles * tm
    pad_start = (tile_base * tm + counts).astype(jnp.int32)
    pad_n = (tiles_e * tm - counts).astype(jnp.int32)
    tile_id = jnp.arange(n_tiles, dtype=jnp.int32)
    tile_expert = jnp.minimum(jnp.searchsorted(tile_end, tile_id, side='right'),
                              n_exp - 1).astype(jnp.int32)
    tile_nsub = jnp.clip(nsub_e[tile_expert] - (tile_id - tile_base[tile_expert]) * MOE_NSUB,
                         0, MOE_NSUB)
    tile_nsub = jnp.where(tile_id < tile_end[-1], tile_nsub, 0).astype(jnp.int32)
    last_e = tile_expert[jnp.maximum(tile_end[-1] - 1, 0)]
    tile_expert = jnp.where(tile_nsub != 0, tile_expert, last_e)
    tile_x = jnp.where(tile_nsub != 0, tile_id, jnp.maximum(tile_end[-1] - 1, 0))

    tq = _pick(math.gcd(*rows), 128)
    slot_tiles = slot.reshape(t // tq, tq, TOP_K).transpose(0, 2, 1).reshape(t // tq, 1,
                                                                              TOP_K * tq)
    tq_d = _pick(t, 512)
    slot_tiles_d = slot.reshape(t // tq_d, tq_d, TOP_K).transpose(0, 2, 1).reshape(
        t // tq_d, 1, TOP_K * tq_d)
    xs = _dispatch(slot_tiles_d, xp, pad_start, pad_n, tile_end[-1:].astype(jnp.int32),
                   n_tiles=n_tiles, tm=tm, tq=tq_d)
    y_sorted = _moe(xs, tile_expert, tile_nsub, tile_x.astype(jnp.int32), w_gu.astype(BF16),
                    b_gu, w_dn.astype(BF16), b_dn, sub=sub, nsub_max=MOE_NSUB,
                    tf=_pick(fdim, 512), tn=_pick(d, 256))
    outs = []
    row0 = 0
    for gi, r in enumerate(rows):
        outs.append(_combine(slot_tiles, y_sorted, h1, gates, final_norm_w, row0=row0, rows=r,
                             tq=tq, name=f"combine{gi}"))
        row0 += r
    return outs


def kernel(x_prompt, x_sample, meta_tokens, norm_mix_w, w_in, lambda_q1, lambda_k1, lambda_q2,
           lambda_k2, attn_subln_w, w_attn_branch, w_fourier_branch, w_out, norm_moe_w,
           w_router, b_router, w_gate_up, b_gate_up, w_down, b_down, final_norm_w):
    depth = norm_mix_w.shape[0]
    assert depth == 1, "meta rows are dropped after the first layer's mixing stage"
    d = x_prompt.shape[-1]
    xs = (x_prompt, x_sample)
    meta_h = jnp.zeros((LANES, d), F32).at[:N_META].set(meta_tokens.astype(F32))

    mix_p = (norm_mix_w[0], w_in[0], lambda_q1[0], lambda_k1[0], lambda_q2[0], lambda_k2[0],
             attn_subln_w[0], w_attn_branch[0], w_fourier_branch[0], w_out[0])
    h1 = _mix_block(xs, meta_h, 0, mix_p)
    moe_p = (norm_moe_w[0], w_router[0], b_router[0], w_gate_up[0], b_gate_up[0], w_down[0],
             b_down[0])
    rows = [x.shape[0] * x.shape[1] for x in xs]
    outs = _moe_block(h1, rows, moe_p, final_norm_w)
    return tuple(o.reshape(x.shape) for o, x in zip(outs, xs))
```

```python
import functools
import math

import jax
import jax.numpy as jnp
from jax import lax
from jax.experimental import pallas as pl
from jax.experimental.pallas import tpu as pltpu

F32 = jnp.float32
BF16 = jnp.bfloat16

N_META = 16
HEAD_DIM = 128
N_FOURIER_GROUPS = 4
ROPE_THETA = 10000.0
TOP_K = 4
SWIGLU_LIMIT = 7.0
SWIGLU_ALPHA = 1.702
EPS = 1e-6
LANES = 128
NEG_BIG = -1e30
LOG2E = math.log2(math.e)
VMEM_LIMIT = 56 * 1024 * 1024
MOE_VMEM_LIMIT = 60 * 1024 * 1024
MOE_SUB = 256
MOE_NSUB = 4


def _cparams(sem, vmem=VMEM_LIMIT, **kw):
    return pltpu.CompilerParams(dimension_semantics=sem, vmem_limit_bytes=vmem, **kw)


def _pick(n, pref):
    t = min(pref, n)
    while n % t:
        t //= 2
    return t


def _mm_kernel(a_ref, b_ref, o_ref, *, precision):
    o_ref[...] = jnp.dot(a_ref[...], b_ref[...], preferred_element_type=F32,
                         precision=precision).astype(o_ref.dtype)


def _matmul(a, b, out_dtype, *, tm, tn, precision=None, name):
    m, k = a.shape
    n = b.shape[1]
    return pl.pallas_call(
        functools.partial(_mm_kernel, precision=precision),
        out_shape=jax.ShapeDtypeStruct((m, n), out_dtype),
        grid=(m // tm, n // tn),
        in_specs=[pl.BlockSpec((tm, k), lambda i, j: (i, 0)),
                  pl.BlockSpec((k, tn), lambda i, j: (0, j))],
        out_specs=pl.BlockSpec((tm, tn), lambda i, j: (i, j)),
        compiler_params=_cparams(("parallel", "arbitrary")),
        name=name,
    )(a, b)


def _inproj_kernel(x_ref, nw_ref, w_ref, cos_ref, sin_ref, o_ref, xn_ref, *, nq, nk, jg, tn,
                   scale):
    j = pl.program_id(1)

    @pl.when(j == 0)
    def _():
        x = x_ref[...]
        ms = jnp.mean(x * x, axis=-1, keepdims=True)
        xn_ref[...] = (x * lax.rsqrt(ms + EPS) * nw_ref[...]).astype(BF16)

    acc = jnp.dot(xn_ref[...], w_ref[...], preferred_element_type=F32)

    def rope_store(s):
        cos = cos_ref[...]
        sin = sin_ref[...]
        for c in range(tn // HEAD_DIM):
            xc = acc[:, c * HEAD_DIM:(c + 1) * HEAD_DIM]
            rc = pltpu.roll(xc, HEAD_DIM // 2, 1)
            y = xc * cos + rc * sin
            if s != 1.0:
                y = y * s
            o_ref[:, c * HEAD_DIM:(c + 1) * HEAD_DIM] = y.astype(o_ref.dtype)

    @pl.when(j < nq)
    def _():
        rope_store(scale)

    @pl.when((j >= nq) & (j < nq + nk))
    def _():
        rope_store(1.0)

    @pl.when((j >= nq + nk) & (j < jg))
    def _():
        o_ref[...] = acc.astype(o_ref.dtype)

    @pl.when(j >= jg)
    def _():
        o_ref[...] = jax.nn.sigmoid(acc).astype(o_ref.dtype)


def _inproj(x, norm_w, w_ext, cos, sin_signed, *, qkw, gate_off, tm, tn, name):
    t, d = x.shape
    nj = w_ext.shape[1] // tn
    kern = functools.partial(_inproj_kernel, nq=qkw // tn, nk=qkw // tn, jg=gate_off // tn,
                             tn=tn, scale=HEAD_DIM ** -0.5 * LOG2E)
    return pl.pallas_call(
        kern,
        out_shape=jax.ShapeDtypeStruct((t, nj * tn), BF16),
        grid=(t // tm, nj),
        in_specs=[pl.BlockSpec((tm, d), lambda i, j: (i, 0), pipeline_mode=pl.Buffered(1)),
                  pl.BlockSpec((1, d), lambda i, j: (0, 0)),
                  pl.BlockSpec((d, tn), lambda i, j: (0, j)),
                  pl.BlockSpec((tm, HEAD_DIM), lambda i, j: (i, 0)),
                  pl.BlockSpec((tm, HEAD_DIM), lambda i, j: (i, 0))],
        out_specs=pl.BlockSpec((tm, tn), lambda i, j: (i, j)),
        scratch_shapes=[pltpu.VMEM((tm, d), BF16)],
        compiler_params=_cparams(("parallel", "arbitrary")),
        name=name,
    )(x, norm_w.reshape(1, d), w_ext, cos, sin_signed)


def _attn_kernel(q_ref, k_ref, v_ref, km_ref, vm_ref, lq1_ref, lk1_ref, lq2_ref, lk2_ref,
                 sw_ref, prev_ref, o_ref, *, lam_init, kc):
    del prev_ref
    d = HEAD_DIM
    lam = (jnp.exp(jnp.sum(lq1_ref[...] * lk1_ref[...], axis=-1, keepdims=True))
           - jnp.exp(jnp.sum(lq2_ref[...] * lk2_ref[...], axis=-1, keepdims=True))
           + lam_init)
    nt = (((1,), (1,)), ((), ()))
    tq = q_ref.shape[0]
    s_len = k_ref.shape[0]
    mrows = km_ref.shape[0]
    meta_bias = jnp.where(lax.broadcasted_iota(jnp.int32, (tq, mrows), 1) < N_META, 0.0, NEG_BIG)
    chunks = [(km_ref, vm_ref, 0, mrows, True)]
    chunks += [(k_ref, v_ref, c * kc, kc, False) for c in range(s_len // kc)]

    qs = [q_ref[:, m * d:(m + 1) * d] for m in range(2)]
    mx = [None, None]
    l = [None, None]
    acc = [None, None]
    for (kr, vr, off, n, is_meta) in chunks:
        v = vr[off:off + n, :]
        for m in range(2):
            s = lax.dot_general(qs[m], kr[off:off + n, m * d:(m + 1) * d], nt,
                                preferred_element_type=F32)
            if is_meta:
                s = s + meta_bias
            cmx = jnp.max(s, axis=-1, keepdims=True)
            if mx[m] is None:
                mx[m] = cmx
                p = jnp.exp2(s - cmx)
                l[m] = jnp.sum(p, axis=-1, keepdims=True)
                acc[m] = jnp.dot(p.astype(BF16), v, preferred_element_type=F32)
            else:
                mn = jnp.maximum(mx[m], cmx)
                alpha = jnp.exp2(mx[m] - mn)
                p = jnp.exp2(s - mn)
                l[m] = alpha * l[m] + jnp.sum(p, axis=-1, keepdims=True)
                acc[m] = alpha * acc[m] + jnp.dot(p.astype(BF16), v,
                                                  preferred_element_type=F32)
                mx[m] = mn

    o = acc[0] * (1.0 / l[0]) - acc[1] * (lam / l[1])
    ms = jnp.mean(o * o, axis=-1, keepdims=True)
    o = o * lax.rsqrt(ms + EPS) * sw_ref[...] * (1.0 - lam_init)
    o_ref[...] = o.astype(o_ref.dtype)


def _attention(proj, proj_meta, lam_vecs, subln_w, prev, *, row0, nb, s, n_heads, qkw, lam_init,
               tq, kc, name):
    hw = 2 * HEAD_DIM
    kcol = qkw // hw
    vcol = 2 * qkw // hw
    ob = row0 // tq
    mrows = proj_meta.shape[0]
    vec = lambda: pl.BlockSpec((1, HEAD_DIM), lambda b, h, i: (0, 0))
    in_specs = [pl.BlockSpec((tq, hw), lambda b, h, i: (b * (s // tq) + i, h)),
                pl.BlockSpec((s, hw), lambda b, h, i: (b, kcol + h)),
                pl.BlockSpec((s, hw), lambda b, h, i: (b, vcol + h)),
                pl.BlockSpec((mrows, hw), lambda b, h, i: (0, kcol + h)),
                pl.BlockSpec((mrows, hw), lambda b, h, i: (0, vcol + h)),
                vec(), vec(), vec(), vec(),
                pl.BlockSpec((1, hw), lambda b, h, i: (0, 0)),
                pl.BlockSpec(memory_space=pl.ANY)]
    args = [proj, proj, proj, proj_meta, proj_meta, *lam_vecs, subln_w.reshape(1, hw), prev]
    return pl.pallas_call(
        functools.partial(_attn_kernel, lam_init=lam_init, kc=kc),
        out_shape=jax.ShapeDtypeStruct(prev.shape, prev.dtype),
        grid=(nb, n_heads, s // tq),
        in_specs=in_specs,
        out_specs=pl.BlockSpec((tq, hw), lambda b, h, i: (ob + b * (s // tq) + i, h)),
        input_output_aliases={len(args) - 1: 0},
        compiler_params=_cparams(("parallel", "parallel", "arbitrary")),
        name=name,
    )(*args)


def _fourier_kernel(ac_ref, as_ref, uc_ref, us_ref, mc_ref, msn_ref, ucm_ref, usm_ref, prev_ref,
                    o_ref):
    del prev_ref
    acc = jnp.dot(ac_ref[...], uc_ref[...], preferred_element_type=F32)
    acc += jnp.dot(as_ref[...], us_ref[...], preferred_element_type=F32)
    acc += jnp.dot(mc_ref[...], ucm_ref[...], preferred_element_type=F32)
    acc += jnp.dot(msn_ref[...], usm_ref[...], preferred_element_type=F32)
    o_ref[...] = acc.astype(o_ref.dtype)


def _fourier(proj, proj_meta, tabs, prev, *, row0, nb, s, fw, uc_off, tm, tn, name):
    a_c, a_s, m_c, m_s = tabs
    mrows = proj_meta.shape[0]
    ob = row0 // tm
    ucb = uc_off // tn
    usb = (uc_off + fw) // tn
    in_specs = [pl.BlockSpec((tm, s), lambda b, i, j: (i, 0)),
                pl.BlockSpec((tm, s), lambda b, i, j: (i, 0)),
                pl.BlockSpec((s, tn), lambda b, i, j: (b, ucb + j)),
                pl.BlockSpec((s, tn), lambda b, i, j: (b, usb + j)),
                pl.BlockSpec((tm, mrows), lambda b, i, j: (i, 0)),
                pl.BlockSpec((tm, mrows), lambda b, i, j: (i, 0)),
                pl.BlockSpec((mrows, tn), lambda b, i, j: (0, ucb + j)),
                pl.BlockSpec((mrows, tn), lambda b, i, j: (0, usb + j)),
                pl.BlockSpec(memory_space=pl.ANY)]
    args = [a_c, a_s, proj, proj, m_c, m_s, proj_meta, proj_meta, prev]
    return pl.pallas_call(
        _fourier_kernel,
        out_shape=jax.ShapeDtypeStruct(prev.shape, prev.dtype),
        grid=(nb, s // tm, fw // tn),
        in_specs=in_specs,
        out_specs=pl.BlockSpec((tm, tn), lambda b, i, j: (ob + b * (s // tm) + i, j)),
        input_output_aliases={len(args) - 1: 0},
        compiler_params=_cparams(("parallel", "parallel", "arbitrary")),
        name=name,
    )(*args)


def _dft_tables(s):
    length = s + N_META
    pos_r = jnp.arange(s, dtype=jnp.int32) + N_META
    pos_m = jnp.where(jnp.arange(LANES) < N_META, jnp.arange(LANES, dtype=jnp.int32), 0)
    w = 2.0 * math.pi / length
    nrm = 1.0 / math.sqrt(length)

    def tab(pc, mask=None):
        ang = ((pos_r[:, None] * pc[None, :]) % length).astype(F32) * w
        c = jnp.cos(ang) * nrm
        sn = jnp.sin(ang) * nrm
        if mask is not None:
            c = jnp.where(mask[None, :], c, 0.0)
            sn = jnp.where(mask[None, :], sn, 0.0)
        return c.astype(BF16), sn.astype(BF16)

    a_c, a_s = tab(pos_r)
    m_c, m_s = tab(pos_m, jnp.arange(LANES) < N_META)
    return a_c, a_s, m_c, m_s


def _two_source(n0, nj):
    first = lambda col: (lambda i, j: (jnp.minimum(i, n0 - 1),
                                       col(jnp.where(i < n0, j, nj - 1))))
    second = lambda col: (lambda i, j: (jnp.maximum(i - n0, 0), col(jnp.where(i < n0, 0, j))))
    return first, second


def _merge_kernel(a_ref, f_ref, wa_ref, wf_ref, g0a_ref, g1a_ref, g0b_ref, g1b_ref, o_ref, *,
                  n0):
    ab = jnp.dot(a_ref[...], wa_ref[...], preferred_element_type=F32)
    fb = jnp.dot(f_ref[...], wf_ref[...], preferred_element_type=F32)
    i = pl.program_id(0)

    @pl.when(i < n0)
    def _():
        m = g0a_ref[...].astype(F32) * ab + g1a_ref[...].astype(F32) * fb
        o_ref[...] = m.astype(o_ref.dtype)

    @pl.when(i >= n0)
    def _():
        m = g0b_ref[...].astype(F32) * ab + g1b_ref[...].astype(F32) * fb
        o_ref[...] = m.astype(o_ref.dtype)


def _merge(attn, fmix, w_ab, w_fb, projs, *, gate_off, tm, tn):
    t, vw = attn.shape
    fw = fmix.shape[1]
    d = w_ab.shape[1]
    nj = d // tn
    n0 = projs[0].shape[0] // tm
    g0b = gate_off // tn
    g1b = (gate_off + d) // tn
    first, second = _two_source(n0, nj)
    return pl.pallas_call(
        functools.partial(_merge_kernel, n0=n0),
        out_shape=jax.ShapeDtypeStruct((t, d), BF16),
        grid=(t // tm, nj),
        in_specs=[pl.BlockSpec((tm, vw), lambda i, j: (i, 0)),
                  pl.BlockSpec((tm, fw), lambda i, j: (i, 0)),
                  pl.BlockSpec((vw, tn), lambda i, j: (0, j)),
                  pl.BlockSpec((fw, tn), lambda i, j: (0, j)),
                  pl.BlockSpec((tm, tn), first(lambda j: g0b + j)),
                  pl.BlockSpec((tm, tn), first(lambda j: g1b + j)),
                  pl.BlockSpec((tm, tn), second(lambda j: g0b + j)),
                  pl.BlockSpec((tm, tn), second(lambda j: g1b + j))],
        out_specs=pl.BlockSpec((tm, tn), lambda i, j: (i, j)),
        compiler_params=_cparams(("parallel", "arbitrary")),
        name="merge",
    )(attn, fmix, w_ab, w_fb, projs[0], projs[0], projs[1], projs[1])


def _outproj_kernel(m_ref, w_ref, ha_ref, hb_ref, o_ref, *, n0):
    acc = jnp.dot(m_ref[...], w_ref[...], preferred_element_type=F32)
    i = pl.program_id(0)

    @pl.when(i < n0)
    def _():
        o_ref[...] = ha_ref[...] + acc

    @pl.when(i >= n0)
    def _():
        o_ref[...] = hb_ref[...] + acc


def _outproj(merged, w_out, hs, *, tm, tn):
    t, d = merged.shape
    nj = d // tn
    n0 = hs[0].shape[0] // tm
    first, second = _two_source(n0, nj)
    return pl.pallas_call(
        functools.partial(_outproj_kernel, n0=n0),
        out_shape=jax.ShapeDtypeStruct((t, d), F32),
        grid=(t // tm, nj),
        in_specs=[pl.BlockSpec((tm, d), lambda i, j: (i, 0)),
                  pl.BlockSpec((d, tn), lambda i, j: (0, j)),
                  pl.BlockSpec((tm, tn), first(lambda j: j)),
                  pl.BlockSpec((tm, tn), second(lambda j: j))],
        out_specs=pl.BlockSpec((tm, tn), lambda i, j: (i, j)),
        compiler_params=_cparams(("parallel", "arbitrary")),
        name="outproj",
    )(merged, w_out, hs[0], hs[1])


def _pack_bf16_pairs(x):
    n = x.shape[1] // 2
    bits = lax.bitcast_convert_type(x.astype(BF16).astype(F32), jnp.uint32)
    return (bits[:, :n] >> 16) | (bits[:, n:] & jnp.uint32(0xFFFF0000))


def _unpack_bf16_pairs(w):
    lo = lax.bitcast_convert_type(w << 16, F32)
    hi = lax.bitcast_convert_type(w & jnp.uint32(0xFFFF0000), F32)
    return lo.astype(BF16), hi.astype(BF16)


def _router_kernel(h_ref, nw_ref, wh_ref, wl_ref, br_ref, o_ref, cnt_ref, xp_ref, carry_ref, *,
                   top_k):
    i = pl.program_id(0)

    @pl.when(i == 0)
    def _():
        carry_ref[...] = jnp.zeros_like(carry_ref)

    x = h_ref[...]
    tr = x.shape[0]
    ms = jnp.mean(x * x, axis=-1, keepdims=True)
    xn = x * lax.rsqrt(ms + EPS) * nw_ref[...]
    xp_ref[...] = _pack_bf16_pairs(xn)
    xh = xn.astype(BF16)
    xl = (xn - xh.astype(F32)).astype(BF16)
    logits = (jnp.dot(xh, wh_ref[...], preferred_element_type=F32)
              + jnp.dot(xh, wl_ref[...], preferred_element_type=F32)
              + jnp.dot(xl, wh_ref[...], preferred_element_type=F32)) + br_ref[...]
    lane = lax.broadcasted_iota(jnp.int32, logits.shape, 1)
    work = logits
    idxs, vals = [], []
    sel = jnp.zeros(logits.shape, F32)
    for _ in range(top_k):
        mx = jnp.max(work, axis=-1, keepdims=True)
        ix = jnp.min(jnp.where(work == mx, lane, LANES), axis=-1, keepdims=True)
        hit = lane == ix
        sel = jnp.where(hit, 1.0, sel)
        work = jnp.where(hit, -jnp.inf, work)
        idxs.append(ix)
        vals.append(mx)
    ex = [jnp.exp(v - vals[0]) for v in vals]
    den = ex[0]
    for e in ex[1:]:
        den = den + e
    inv = 1.0 / den
    r_i = lax.broadcasted_iota(jnp.int32, (tr, tr), 0)
    c_i = lax.broadcasted_iota(jnp.int32, (tr, tr), 1)
    tril = jnp.where(c_i < r_i, 1.0, 0.0).astype(BF16)
    rank_dense = jnp.dot(tril, sel.astype(BF16), preferred_element_type=F32) + carry_ref[...]
    out = jnp.zeros(logits.shape, F32)
    for k in range(top_k):
        rk = jnp.sum(jnp.where(lane == idxs[k], rank_dense, 0.0), axis=-1, keepdims=True)
        out = jnp.where(lane == k, idxs[k].astype(F32), out)
        out = jnp.where(lane == top_k + k, ex[k] * inv, out)
        out = jnp.where(lane == 2 * top_k + k, rk, out)
    o_ref[...] = out
    carry_ref[...] += jnp.sum(sel, axis=0, keepdims=True)
    cnt_ref[...] = carry_ref[...]


def _router(h1, norm_w, w_router, b_router, *, tr):
    t, d = h1.shape
    e = w_router.shape[1]
    wr = jnp.zeros((d, LANES), F32).at[:, :e].set(w_router)
    wr_hi = wr.astype(BF16)
    wr_lo = (wr - wr_hi.astype(F32)).astype(BF16)
    br = jnp.full((1, LANES), NEG_BIG, F32).at[0, :e].set(b_router)
    return pl.pallas_call(
        functools.partial(_router_kernel, top_k=TOP_K),
        out_shape=(jax.ShapeDtypeStruct((t, LANES), F32),
                   jax.ShapeDtypeStruct((1, LANES), F32),
                   jax.ShapeDtypeStruct((t, d // 2), jnp.uint32)),
        grid=(t // tr,),
        in_specs=[pl.BlockSpec((tr, d), lambda i: (i, 0)),
                  pl.BlockSpec((1, d), lambda i: (0, 0)),
                  pl.BlockSpec((d, LANES), lambda i: (0, 0)),
                  pl.BlockSpec((d, LANES), lambda i: (0, 0)),
                  pl.BlockSpec((1, LANES), lambda i: (0, 0))],
        out_specs=(pl.BlockSpec((tr, LANES), lambda i: (i, 0)),
                   pl.BlockSpec((1, LANES), lambda i: (0, 0)),
                   pl.BlockSpec((tr, d // 2), lambda i: (i, 0))),
        scratch_shapes=[pltpu.VMEM((1, LANES), F32)],
        compiler_params=_cparams(("arbitrary",)),
        name="router",
    )(h1, norm_w.reshape(1, d), wr_hi, wr_lo, br)


def _gather_start(idx_ref, idx0, src_hbm, dst_ref, sem, n):
    def issue(r, c):
        pltpu.make_async_copy(src_hbm.at[pl.ds(idx_ref[idx0 + r], 1)], dst_ref.at[pl.ds(r, 1)],
                              sem).start()
        return c

    lax.fori_loop(0, n, issue, 0, unroll=8)


def _gather_wait(src_hbm, dst_ref, sem, n):
    pltpu.make_async_copy(src_hbm.at[pl.ds(0, n)], dst_ref.at[pl.ds(0, n)], sem).wait()


def _dispatch_kernel(ps_ref, pn_ref, used_ref, slot_hbm, xp_ref, xs_hbm, slot_smem, zero_ref,
                     sem_i, sem_g, sem_z, *, top_k, tq, n_exp, n_tiles):
    i = pl.program_id(0)
    n = pl.num_programs(0)
    cp = pltpu.make_async_copy(slot_hbm.at[i], slot_smem, sem_i)
    cp.start()
    cp.wait()

    def issue(r, c):
        src = xp_ref.at[pl.ds(r, 1)]
        for k in range(top_k):
            pltpu.make_async_copy(src, xs_hbm.at[pl.ds(slot_smem[0, k * tq + r], 1)],
                                  sem_g).start()
        return c

    lax.fori_loop(0, tq, issue, 0, unroll=2)
    for k in range(top_k):
        pltpu.make_async_copy(xp_ref, xs_hbm.at[pl.ds(0, tq)], sem_g).wait()

    @pl.when(i == n - 1)
    def _():
        zero_ref[...] = jnp.zeros_like(zero_ref)
        zrow = zero_ref.at[pl.ds(0, 1)]

        def per_expert(e, c):
            start = ps_ref[e]
            cnt = pn_ref[e]

            def fill(r, c2):
                pltpu.make_async_copy(zrow, xs_hbm.at[pl.ds(start + r, 1)], sem_z).start()
                return c2

            def done(r, c2):
                pltpu.make_async_copy(zrow, xs_hbm.at[pl.ds(0, 1)], sem_z).wait()
                return c2

            lax.fori_loop(0, cnt, fill, 0)
            lax.fori_loop(0, cnt, done, 0)
            return c

        lax.fori_loop(0, n_exp, per_expert, 0)
        tm = zero_ref.shape[0]

        def fill_tile(j, c):
            cp_z = pltpu.make_async_copy(zero_ref, xs_hbm.at[pl.ds(j * tm, tm)], sem_z)
            cp_z.start()
            cp_z.wait()
            return c

        lax.fori_loop(used_ref[0], n_tiles, fill_tile, 0)


def _dispatch(slot_tiles, xp, pad_start, pad_n, tiles_used, *, n_tiles, tm, tq):
    t, half = xp.shape
    grid_spec = pltpu.PrefetchScalarGridSpec(
        num_scalar_prefetch=3,
        grid=(t // tq,),
        in_specs=[pl.BlockSpec(memory_space=pl.ANY),
                  pl.BlockSpec((tq, half), lambda i, ps, pn, used: (i, 0))],
        out_specs=pl.BlockSpec(memory_space=pl.ANY),
        scratch_shapes=[pltpu.SMEM((1, TOP_K * tq), jnp.int32),
                        pltpu.VMEM((tm, half), jnp.uint32),
                        pltpu.SemaphoreType.DMA,
                        pltpu.SemaphoreType.DMA,
                        pltpu.SemaphoreType.DMA],
    )
    return pl.pallas_call(
        functools.partial(_dispatch_kernel, top_k=TOP_K, tq=tq, n_exp=pad_start.shape[0],
                          n_tiles=n_tiles),
        out_shape=jax.ShapeDtypeStruct((n_tiles * tm, half), jnp.uint32),
        grid_spec=grid_spec,
        compiler_params=_cparams(("arbitrary",)),
        name="dispatch",
    )(pad_start, pad_n, tiles_used, slot_tiles, xp)


def _moe_kernel(te_ref, ns_ref, xs_hbm, wg_ref, wu_ref, bg_ref, bu_ref, wd_ref, bd_ref,
                o_ref, xbuf, xn_ref, act_ref, sem_x, *, nf, sub, nsub_max):
    del te_ref
    i = pl.program_id(0)
    s = pl.program_id(1)
    nsub = ns_ref[i]
    half = xbuf.shape[2]
    tf = act_ref.shape[2]
    full = nsub == nsub_max
    part = (nsub > 0) & (nsub < nsub_max)

    def rows(c):
        return pl.ds(pl.multiple_of(c * sub, sub), sub)

    @pl.when((s == 0) & (nsub > 0))
    def _():
        def load(c, b):
            r0 = pl.multiple_of((i * nsub_max + c) * sub, sub)
            return pltpu.make_async_copy(xs_hbm.at[pl.ds(r0, sub)], xbuf.at[b], sem_x.at[b])

        load(0, 0).start()

        def body(c, carry):
            b = c % 2

            @pl.when(c + 1 < nsub)
            def _():
                load(c + 1, 1 - b).start()

            load(c, b).wait()
            lo, hi = _unpack_bf16_pairs(xbuf[b])
            xn_ref[rows(c), 0:half] = lo
            xn_ref[rows(c), half:2 * half] = hi
            return carry

        lax.fori_loop(0, nsub, body, 0)

    def swiglu(xn):
        hg = jnp.dot(xn, wg_ref[...], preferred_element_type=F32) + bg_ref[...]
        hu = jnp.dot(xn, wu_ref[...], preferred_element_type=F32) + bu_ref[...]
        g = jnp.minimum(hg, SWIGLU_LIMIT)
        u = jnp.clip(hu, -SWIGLU_LIMIT, SWIGLU_LIMIT)
        return ((u + 1.0) * (g * jax.nn.sigmoid(SWIGLU_ALPHA * g))).astype(BF16)

    def down(act_of):
        acc = jnp.dot(act_of(0), wd_ref[0:tf, :], preferred_element_type=F32)
        for f in range(1, nf):
            acc += jnp.dot(act_of(f), wd_ref[f * tf:(f + 1) * tf, :],
                           preferred_element_type=F32)
        return acc + bd_ref[...]

    sf = jnp.minimum(s, nf - 1)

    @pl.when((s < nf) & full)
    def _():
        act_ref[sf] = swiglu(xn_ref[...])

    @pl.when((s < nf) & part)
    def _():
        def body(c, carry):
            act_ref[sf, rows(c), :] = swiglu(xn_ref[rows(c), :])
            return carry

        lax.fori_loop(0, nsub, body, 0)

    @pl.when((s >= nf) & full)
    def _():
        o_ref[...] = down(lambda f: act_ref[f])

    @pl.when((s >= nf) & jnp.logical_not(full))
    def _():
        def body(c, carry):
            o_ref[rows(c), :] = down(lambda f: act_ref[f, rows(c), :])
            return carry

        lax.fori_loop(0, nsub, body, 0)

        def zero(c, carry):
            o_ref[rows(c), :] = jnp.zeros((sub, o_ref.shape[1]), o_ref.dtype)
            return carry

        lax.fori_loop(nsub, nsub_max, zero, 0)


def _moe(xs, tile_expert, tile_nsub, w_gu, b_gu, w_dn, b_dn, *, sub, nsub_max, tf, tn):
    half = xs.shape[1]
    d = 2 * half
    n_exp, _, f2 = w_gu.shape
    fdim = f2 // 2
    nf = fdim // tf
    nn = d // tn
    tm = sub * nsub_max
    n_tiles = xs.shape[0] // tm
    last_n = nn - 1

    def ge(i, s, te, ns):
        nxt = jnp.minimum(i + 1, n_tiles - 1)
        return jnp.where((ns[i] != 0) & (s >= nf), te[nxt], te[i])

    def fi(i, s, te, ns):
        return jnp.where((ns[i] != 0) & (s < nf), s, 0)

    def ni(i, s, te, ns):
        return jnp.where(ns[i] != 0, jnp.maximum(s - nf, 0), last_n)

    def oi(i, s, te, ns):
        return jnp.maximum(s - nf, 0)

    b_gu3 = b_gu.reshape(n_exp, 1, f2)
    b_dn3 = b_dn.reshape(n_exp, 1, d)
    grid_spec = pltpu.PrefetchScalarGridSpec(
        num_scalar_prefetch=2,
        grid=(n_tiles, nf + nn),
        in_specs=[pl.BlockSpec(memory_space=pl.ANY),
                  pl.BlockSpec((None, d, tf), lambda *a: (ge(*a), 0, fi(*a))),
                  pl.BlockSpec((None, d, tf), lambda *a: (ge(*a), 0, nf + fi(*a))),
                  pl.BlockSpec((None, 1, tf), lambda *a: (ge(*a), 0, fi(*a))),
                  pl.BlockSpec((None, 1, tf), lambda *a: (ge(*a), 0, nf + fi(*a))),
                  pl.BlockSpec((None, fdim, tn),
                               lambda i, s, te, ns: (te[i], 0, ni(i, s, te, ns))),
                  pl.BlockSpec((None, 1, tn), lambda i, s, te, ns: (te[i], 0, ni(i, s, te, ns)))],
        out_specs=pl.BlockSpec((tm, tn), lambda i, s, te, ns: (i, oi(i, s, te, ns))),
        scratch_shapes=[pltpu.VMEM((2, sub, half), jnp.uint32),
                        pltpu.VMEM((tm, d), BF16),
                        pltpu.VMEM((nf, tm, tf), BF16),
                        pltpu.SemaphoreType.DMA((2,))],
    )
    return pl.pallas_call(
        functools.partial(_moe_kernel, nf=nf, sub=sub, nsub_max=nsub_max),
        out_shape=jax.ShapeDtypeStruct((n_tiles * tm, d), F32),
        grid_spec=grid_spec,
        compiler_params=_cparams(("arbitrary", "arbitrary"), vmem=MOE_VMEM_LIMIT),
        name="moe",
    )(tile_expert, tile_nsub, xs, w_gu, w_gu, b_gu3, b_gu3, w_dn, b_dn3)


def _combine_kernel(slot_hbm, y_hbm, h_ref, g_ref, nw_ref, o_ref, slot_smem, buf, sem_i, sem_g,
                    *, top_k, tile0):
    i = pl.program_id(0)
    n = pl.num_programs(0)
    tq = h_ref.shape[0]
    nrow = top_k * tq

    def fetch(tile, b):
        cp = pltpu.make_async_copy(slot_hbm.at[tile], slot_smem.at[b], sem_i)
        cp.start()
        cp.wait()
        _gather_start(slot_smem.at[b].at[0], 0, y_hbm, buf.at[b], sem_g.at[b], nrow)

    def consume(b):
        _gather_wait(y_hbm, buf.at[b], sem_g.at[b], nrow)
        g = g_ref[...]
        h = h_ref[...]
        for k in range(top_k):
            h = h + g[:, k:k + 1] * buf[b, k * tq:(k + 1) * tq, :]
        ms = jnp.mean(h * h, axis=-1, keepdims=True)
        o_ref[...] = h * lax.rsqrt(ms + EPS) * nw_ref[...]

    @pl.when(i == 0)
    def _():
        fetch(tile0, 0)

    for b in range(2):
        @pl.when(i % 2 == b)
        def _():
            @pl.when(i + 1 < n)
            def _():
                fetch(tile0 + i + 1, 1 - b)

            consume(b)


def _combine(slot_tiles, y_sorted, h1, gates, final_w, *, row0, rows, tq, name):
    d = h1.shape[1]
    kern = functools.partial(_combine_kernel, top_k=TOP_K, tile0=row0 // tq)
    return pl.pallas_call(
        kern,
        out_shape=jax.ShapeDtypeStruct((rows, d), F32),
        grid=(rows // tq,),
        in_specs=[pl.BlockSpec(memory_space=pl.ANY),
                  pl.BlockSpec(memory_space=pl.ANY),
                  pl.BlockSpec((tq, d), lambda i: (row0 // tq + i, 0)),
                  pl.BlockSpec((tq, TOP_K), lambda i: (row0 // tq + i, 0)),
                  pl.BlockSpec((1, d), lambda i: (0, 0))],
        out_specs=pl.BlockSpec((tq, d), lambda i: (i, 0)),
        scratch_shapes=[pltpu.SMEM((2, 1, TOP_K * tq), jnp.int32),
                        pltpu.VMEM((2, TOP_K * tq, d), F32),
                        pltpu.SemaphoreType.DMA,
                        pltpu.SemaphoreType.DMA((2,))],
        compiler_params=_cparams(("arbitrary",)),
        name=name,
    )(slot_tiles, y_sorted, h1, gates, final_w.reshape(1, d))


def _rope_rows(pos):
    inv_freq = 1.0 / (ROPE_THETA ** (jnp.arange(0, HEAD_DIM, 2, dtype=F32) / HEAD_DIM))
    ang = pos.astype(F32)[:, None] * inv_freq[None, :]
    cos = jnp.cos(ang)
    sin = jnp.sin(ang)
    return (jnp.concatenate([cos, cos], axis=-1), jnp.concatenate([-sin, sin], axis=-1))


def _channel_dft(fw):
    gc = fw // N_FOURIER_GROUPS
    idx = jnp.arange(gc, dtype=jnp.int32)
    ang = ((idx[:, None] * idx[None, :]) % gc).astype(F32) * (2.0 * math.pi / gc)
    nrm = 1.0 / math.sqrt(gc)
    eye = jnp.eye(N_FOURIER_GROUPS, dtype=F32)
    return jnp.concatenate([jnp.kron(eye, jnp.cos(ang) * nrm),
                            jnp.kron(eye, -jnp.sin(ang) * nrm)], axis=1)


def _lambda_init(layer):
    return 0.8 - 0.6 * math.exp(-0.3 * layer)


def _mix_block(xs, meta_h, layer, p):
    (norm_mix_w, w_in, lq1, lk1, lq2, lk2, subln_w, w_ab, w_fb, w_out) = p
    d = w_in.shape[0]
    vw, fw = w_ab.shape[0], w_fb.shape[0]
    qkw = (w_in.shape[1] - vw - fw - 2 * d) // 2
    n_heads = vw // (2 * HEAD_DIM)
    u_off = 2 * qkw + vw
    gate_off = u_off + 2 * fw
    lam_init = _lambda_init(layer)
    rows = [x.shape[0] * x.shape[1] for x in xs]
    t = sum(rows)
    tm = _pick(math.gcd(*rows), 1024)

    w_u = w_in[:, u_off:u_off + fw]
    w_ucs = _matmul(w_u, _channel_dft(fw), F32, tm=_pick(d, 512), tn=_pick(2 * fw, 512),
                    precision=lax.Precision.HIGHEST, name="fold")
    tn_in = _pick(math.gcd(qkw, math.gcd(vw, math.gcd(2 * fw, d))), 512)
    w_ext = jnp.concatenate([w_in[:, :u_off].astype(BF16), w_ucs.astype(BF16),
                             w_in[:, u_off + fw:].astype(BF16)], axis=1)

    mt = meta_h.shape[0]
    cos_m, sin_m = _rope_rows(jnp.arange(mt, dtype=jnp.int32))
    proj_meta = _inproj(meta_h, norm_mix_w, w_ext, cos_m, sin_m, qkw=qkw, gate_off=gate_off,
                        tm=mt, tn=tn_in, name="inproj_meta")

    lam_vecs = [v.reshape(1, HEAD_DIM).astype(F32) for v in (lq1, lk1, lq2, lk2)]
    attn = jnp.zeros((t, vw), BF16)
    fmix = jnp.zeros((t, fw), BF16)
    projs, hs = [], []
    row0 = 0
    for gi, x in enumerate(xs):
        nb, s, _ = x.shape
        h = x.reshape(nb * s, d)
        cos, sin_s = _rope_rows(N_META + (jnp.arange(nb * s, dtype=jnp.int32) % s))
        proj = _inproj(h, norm_mix_w, w_ext, cos, sin_s, qkw=qkw, gate_off=gate_off, tm=tm,
                       tn=tn_in, name=f"inproj{gi}")
        attn = _attention(proj, proj_meta, lam_vecs, subln_w, attn, row0=row0, nb=nb, s=s,
                          n_heads=n_heads, qkw=qkw, lam_init=lam_init, tq=_pick(s, 512),
                          kc=_pick(s, 1024), name=f"attn{gi}")
        fmix = _fourier(proj, proj_meta, _dft_tables(s), fmix, row0=row0, nb=nb, s=s, fw=fw,
                        uc_off=u_off, tm=_pick(s, 512), tn=_pick(fw, 512), name=f"fourier{gi}")
        projs.append(proj)
        hs.append(h)
        row0 += nb * s

    tn = _pick(math.gcd(d, gate_off), 512)
    merged = _merge(attn, fmix, w_ab.astype(BF16), w_fb.astype(BF16), projs, gate_off=gate_off,
                    tm=tm, tn=tn)
    return _outproj(merged, w_out.astype(BF16), hs, tm=tm, tn=tn)


def _moe_block(h1, rows, p, final_norm_w):
    t, d = h1.shape
    norm_moe_w, w_router, b_router, w_gu, b_gu, w_dn, b_dn = p
    n_exp = w_router.shape[1]
    fdim = w_dn.shape[1]
    sub = _pick(t, MOE_SUB)
    tm = sub * MOE_NSUB

    rout, cnt, xp = _router(h1, norm_moe_w, w_router, b_router, tr=_pick(t, 256))
    e_idx = rout[:, 0:TOP_K].astype(jnp.int32)
    gates = rout[:, TOP_K:2 * TOP_K]
    rank = rout[:, 2 * TOP_K:3 * TOP_K].astype(jnp.int32)
    counts = cnt[0, :n_exp].astype(jnp.int32)

    nsub_e = (counts + sub - 1) // sub
    tiles_e = (nsub_e + MOE_NSUB - 1) // MOE_NSUB
    tile_end = jnp.cumsum(tiles_e)
    tile_base = tile_end - tiles_e
    slot = tile_base[e_idx] * tm + rank
    n_tiles = -(-t * TOP_K // tm) + n_exp
    a_pad = n_tiles * tm
    pad_start = (tile_base * tm + counts).astype(jnp.int32)
    pad_n = (tiles_e * tm - counts).astype(jnp.int32)
    tile_id = jnp.arange(n_tiles, dtype=jnp.int32)
    tile_expert = jnp.minimum(jnp.searchsorted(tile_end, tile_id, side='right'),
                              n_exp - 1).astype(jnp.int32)
    tile_nsub = jnp.clip(nsub_e[tile_expert] - (tile_id - tile_base[tile_expert]) * MOE_NSUB,
                         0, MOE_NSUB)
    tile_nsub = jnp.where(tile_id < tile_end[-1], tile_nsub, 0).astype(jnp.int32)
    last_e = tile_expert[jnp.maximum(tile_end[-1] - 1, 0)]
    tile_expert = jnp.where(tile_nsub != 0, tile_expert, last_e)

    tq = _pick(math.gcd(*rows), 128)
    slot_tiles = slot.reshape(t // tq, tq, TOP_K).transpose(0, 2, 1).reshape(t // tq, 1,
                                                                              TOP_K * tq)
    tq_d = _pick(t, 512)
    slot_tiles_d = slot.reshape(t // tq_d, tq_d, TOP_K).transpose(0, 2, 1).reshape(
        t // tq_d, 1, TOP_K * tq_d)
    xs = _dispatch(slot_tiles_d, xp, pad_start, pad_n, tile_end[-1:].astype(jnp.int32),
                   n_tiles=n_tiles, tm=tm, tq=tq_d)
    y_sorted = _moe(xs, tile_expert, tile_nsub, w_gu.astype(BF16), b_gu, w_dn.astype(BF16), b_dn,
                    sub=sub, nsub_max=MOE_NSUB, tf=_pick(fdim, 512), tn=_pick(d, 512))
    outs = []
    row0 = 0
    for gi, r in enumerate(rows):
        outs.append(_combine(slot_tiles, y_sorted, h1, gates, final_norm_w, row0=row0, rows=r,
                             tq=tq, name=f"combine{gi}"))
        row0 += r
    return outs


def kernel(x_prompt, x_sample, meta_tokens, norm_mix_w, w_in, lambda_q1, lambda_k1, lambda_q2,
           lambda_k2, attn_subln_w, w_attn_branch, w_fourier_branch, w_out, norm_moe_w,
           w_router, b_router, w_gate_up, b_gate_up, w_down, b_down, final_norm_w):
    depth = norm_mix_w.shape[0]
    assert depth == 1, "meta rows are dropped after the first layer's mixing stage"
    d = x_prompt.shape[-1]
    xs = (x_prompt, x_sample)
    meta_h = jnp.zeros((LANES, d), F32).at[:N_META].set(meta_tokens.astype(F32))

    mix_p = (norm_mix_w[0], w_in[0], lambda_q1[0], lambda_k1[0], lambda_q2[0], lambda_k2[0],
             attn_subln_w[0], w_attn_branch[0], w_fourier_branch[0], w_out[0])
    h1 = _mix_block(xs, meta_h, 0, mix_p)
    moe_p = (norm_moe_w[0], w_router[0], b_router[0], w_gate_up[0], b_gate_up[0], w_down[0],
             b_down[0])
    rows = [x.shape[0] * x.shape[1] for x in xs]
    outs = _moe_block(h1, rows, moe_p, final_norm_w)
    return tuple(o.reshape(x.shape) for o, x in zip(outs, xs))
```

```python
import functools
import math

import jax
import jax.numpy as jnp
from jax import lax
from jax.experimental import pallas as pl
from jax.experimental.pallas import tpu as pltpu

F32 = jnp.float32
BF16 = jnp.bfloat16

N_META = 16
HEAD_DIM = 128
N_FOURIER_GROUPS = 4
ROPE_THETA = 10000.0
TOP_K = 4
SWIGLU_LIMIT = 7.0
SWIGLU_ALPHA = 1.702
EPS = 1e-6
LANES = 128
NEG_BIG = -1e30
LOG2E = math.log2(math.e)
VMEM_LIMIT = 56 * 1024 * 1024
MOE_VMEM_LIMIT = 60 * 1024 * 1024
MOE_SUB = 256
MOE_NSUB = 4


def _cparams(sem, vmem=VMEM_LIMIT, **kw):
    return pltpu.CompilerParams(dimension_semantics=sem, vmem_limit_bytes=vmem, **kw)


def _pick(n, pref):
    t = min(pref, n)
    while n % t:
        t //= 2
    return t


def _mm_kernel(a_ref, b_ref, o_ref, *, precision):
    o_ref[...] = jnp.dot(a_ref[...], b_ref[...], preferred_element_type=F32,
                         precision=precision).astype(o_ref.dtype)


def _matmul(a, b, out_dtype, *, tm, tn, precision=None, name):
    m, k = a.shape
    n = b.shape[1]
    return pl.pallas_call(
        functools.partial(_mm_kernel, precision=precision),
        out_shape=jax.ShapeDtypeStruct((m, n), out_dtype),
        grid=(m // tm, n // tn),
        in_specs=[pl.BlockSpec((tm, k), lambda i, j: (i, 0)),
                  pl.BlockSpec((k, tn), lambda i, j: (0, j))],
        out_specs=pl.BlockSpec((tm, tn), lambda i, j: (i, j)),
        compiler_params=_cparams(("parallel", "arbitrary")),
        name=name,
    )(a, b)


def _inproj_kernel(x_ref, nw_ref, w_ref, cos_ref, sin_ref, o_ref, xn_ref, *, nq, nk, jg, tn,
                   scale):
    j = pl.program_id(1)

    @pl.when(j == 0)
    def _():
        x = x_ref[...]
        ms = jnp.mean(x * x, axis=-1, keepdims=True)
        xn_ref[...] = (x * lax.rsqrt(ms + EPS) * nw_ref[...]).astype(BF16)

    acc = jnp.dot(xn_ref[...], w_ref[...], preferred_element_type=F32)

    def rope_store(s):
        cos = cos_ref[...]
        sin = sin_ref[...]
        for c in range(tn // HEAD_DIM):
            xc = acc[:, c * HEAD_DIM:(c + 1) * HEAD_DIM]
            rc = pltpu.roll(xc, HEAD_DIM // 2, 1)
            y = xc * cos + rc * sin
            if s != 1.0:
                y = y * s
            o_ref[:, c * HEAD_DIM:(c + 1) * HEAD_DIM] = y.astype(o_ref.dtype)

    @pl.when(j < nq)
    def _():
        rope_store(scale)

    @pl.when((j >= nq) & (j < nq + nk))
    def _():
        rope_store(1.0)

    @pl.when((j >= nq + nk) & (j < jg))
    def _():
        o_ref[...] = acc.astype(o_ref.dtype)

    @pl.when(j >= jg)
    def _():
        o_ref[...] = jax.nn.sigmoid(acc).astype(o_ref.dtype)


def _inproj(x, norm_w, w_ext, cos, sin_signed, *, qkw, gate_off, tm, tn, name):
    t, d = x.shape
    nj = w_ext.shape[1] // tn
    kern = functools.partial(_inproj_kernel, nq=qkw // tn, nk=qkw // tn, jg=gate_off // tn,
                             tn=tn, scale=HEAD_DIM ** -0.5 * LOG2E)
    return pl.pallas_call(
        kern,
        out_shape=jax.ShapeDtypeStruct((t, nj * tn), BF16),
        grid=(t // tm, nj),
        in_specs=[pl.BlockSpec((tm, d), lambda i, j: (i, 0), pipeline_mode=pl.Buffered(1)),
                  pl.BlockSpec((1, d), lambda i, j: (0, 0)),
                  pl.BlockSpec((d, tn), lambda i, j: (0, j)),
                  pl.BlockSpec((tm, HEAD_DIM), lambda i, j: (i, 0)),
                  pl.BlockSpec((tm, HEAD_DIM), lambda i, j: (i, 0))],
        out_specs=pl.BlockSpec((tm, tn), lambda i, j: (i, j)),
        scratch_shapes=[pltpu.VMEM((tm, d), BF16)],
        compiler_params=_cparams(("parallel", "arbitrary")),
        name=name,
    )(x, norm_w.reshape(1, d), w_ext, cos, sin_signed)


def _attn_kernel(q_ref, k_ref, v_ref, km_ref, vm_ref, lq1_ref, lk1_ref, lq2_ref, lk2_ref,
                 sw_ref, prev_ref, o_ref, *, lam_init, kc):
    del prev_ref
    d = HEAD_DIM
    lam = (jnp.exp(jnp.sum(lq1_ref[...] * lk1_ref[...], axis=-1, keepdims=True))
           - jnp.exp(jnp.sum(lq2_ref[...] * lk2_ref[...], axis=-1, keepdims=True))
           + lam_init)
    nt = (((1,), (1,)), ((), ()))
    tq = q_ref.shape[0]
    s_len = k_ref.shape[0]
    mrows = km_ref.shape[0]
    meta_bias = jnp.where(lax.broadcasted_iota(jnp.int32, (tq, mrows), 1) < N_META, 0.0, NEG_BIG)
    chunks = [(km_ref, vm_ref, 0, mrows, True)]
    chunks += [(k_ref, v_ref, c * kc, kc, False) for c in range(s_len // kc)]

    qs = [q_ref[:, m * d:(m + 1) * d] for m in range(2)]
    mx = [None, None]
    l = [None, None]
    acc = [None, None]
    for (kr, vr, off, n, is_meta) in chunks:
        v = vr[off:off + n, :]
        for m in range(2):
            s = lax.dot_general(qs[m], kr[off:off + n, m * d:(m + 1) * d], nt,
                                preferred_element_type=F32)
            if is_meta:
                s = s + meta_bias
            cmx = jnp.max(s, axis=-1, keepdims=True)
            if mx[m] is None:
                mx[m] = cmx
                p = jnp.exp2(s - cmx)
                l[m] = jnp.sum(p, axis=-1, keepdims=True)
                acc[m] = jnp.dot(p.astype(BF16), v, preferred_element_type=F32)
            else:
                mn = jnp.maximum(mx[m], cmx)
                alpha = jnp.exp2(mx[m] - mn)
                p = jnp.exp2(s - mn)
                l[m] = alpha * l[m] + jnp.sum(p, axis=-1, keepdims=True)
                acc[m] = alpha * acc[m] + jnp.dot(p.astype(BF16), v,
                                                  preferred_element_type=F32)
                mx[m] = mn

    o = acc[0] * (1.0 / l[0]) - acc[1] * (lam / l[1])
    ms = jnp.mean(o * o, axis=-1, keepdims=True)
    o = o * lax.rsqrt(ms + EPS) * sw_ref[...] * (1.0 - lam_init)
    o_ref[...] = o.astype(o_ref.dtype)


def _attention(proj, proj_meta, lam_vecs, subln_w, prev, *, row0, nb, s, n_heads, qkw, lam_init,
               tq, kc, name):
    hw = 2 * HEAD_DIM
    kcol = qkw // hw
    vcol = 2 * qkw // hw
    ob = row0 // tq
    mrows = proj_meta.shape[0]
    vec = lambda: pl.BlockSpec((1, HEAD_DIM), lambda b, h, i: (0, 0))
    in_specs = [pl.BlockSpec((tq, hw), lambda b, h, i: (b * (s // tq) + i, h)),
                pl.BlockSpec((s, hw), lambda b, h, i: (b, kcol + h)),
                pl.BlockSpec((s, hw), lambda b, h, i: (b, vcol + h)),
                pl.BlockSpec((mrows, hw), lambda b, h, i: (0, kcol + h)),
                pl.BlockSpec((mrows, hw), lambda b, h, i: (0, vcol + h)),
                vec(), vec(), vec(), vec(),
                pl.BlockSpec((1, hw), lambda b, h, i: (0, 0)),
                pl.BlockSpec(memory_space=pl.ANY)]
    args = [proj, proj, proj, proj_meta, proj_meta, *lam_vecs, subln_w.reshape(1, hw), prev]
    return pl.pallas_call(
        functools.partial(_attn_kernel, lam_init=lam_init, kc=kc),
        out_shape=jax.ShapeDtypeStruct(prev.shape, prev.dtype),
        grid=(nb, n_heads, s // tq),
        in_specs=in_specs,
        out_specs=pl.BlockSpec((tq, hw), lambda b, h, i: (ob + b * (s // tq) + i, h)),
        input_output_aliases={len(args) - 1: 0},
        compiler_params=_cparams(("parallel", "parallel", "arbitrary")),
        name=name,
    )(*args)


def _fourier_kernel(ac_ref, as_ref, uc_ref, us_ref, mc_ref, msn_ref, ucm_ref, usm_ref, prev_ref,
                    o_ref):
    del prev_ref
    acc = jnp.dot(ac_ref[...], uc_ref[...], preferred_element_type=F32)
    acc += jnp.dot(as_ref[...], us_ref[...], preferred_element_type=F32)
    acc += jnp.dot(mc_ref[...], ucm_ref[...], preferred_element_type=F32)
    acc += jnp.dot(msn_ref[...], usm_ref[...], preferred_element_type=F32)
    o_ref[...] = acc.astype(o_ref.dtype)


def _fourier(proj, proj_meta, tabs, prev, *, row0, nb, s, fw, uc_off, tm, tn, name):
    a_c, a_s, m_c, m_s = tabs
    mrows = proj_meta.shape[0]
    ob = row0 // tm
    ucb = uc_off // tn
    usb = (uc_off + fw) // tn
    in_specs = [pl.BlockSpec((tm, s), lambda b, i, j: (i, 0)),
                pl.BlockSpec((tm, s), lambda b, i, j: (i, 0)),
                pl.BlockSpec((s, tn), lambda b, i, j: (b, ucb + j)),
                pl.BlockSpec((s, tn), lambda b, i, j: (b, usb + j)),
                pl.BlockSpec((tm, mrows), lambda b, i, j: (i, 0)),
                pl.BlockSpec((tm, mrows), lambda b, i, j: (i, 0)),
                pl.BlockSpec((mrows, tn), lambda b, i, j: (0, ucb + j)),
                pl.BlockSpec((mrows, tn), lambda b, i, j: (0, usb + j)),
                pl.BlockSpec(memory_space=pl.ANY)]
    args = [a_c, a_s, proj, proj, m_c, m_s, proj_meta, proj_meta, prev]
    return pl.pallas_call(
        _fourier_kernel,
        out_shape=jax.ShapeDtypeStruct(prev.shape, prev.dtype),
        grid=(nb, s // tm, fw // tn),
        in_specs=in_specs,
        out_specs=pl.BlockSpec((tm, tn), lambda b, i, j: (ob + b * (s // tm) + i, j)),
        input_output_aliases={len(args) - 1: 0},
        compiler_params=_cparams(("parallel", "parallel", "arbitrary")),
        name=name,
    )(*args)


def _dft_tables(s):
    length = s + N_META
    pos_r = jnp.arange(s, dtype=jnp.int32) + N_META
    pos_m = jnp.where(jnp.arange(LANES) < N_META, jnp.arange(LANES, dtype=jnp.int32), 0)
    w = 2.0 * math.pi / length
    nrm = 1.0 / math.sqrt(length)

    def tab(pc, mask=None):
        ang = ((pos_r[:, None] * pc[None, :]) % length).astype(F32) * w
        c = jnp.cos(ang) * nrm
        sn = jnp.sin(ang) * nrm
        if mask is not None:
            c = jnp.where(mask[None, :], c, 0.0)
            sn = jnp.where(mask[None, :], sn, 0.0)
        return c.astype(BF16), sn.astype(BF16)

    a_c, a_s = tab(pos_r)
    m_c, m_s = tab(pos_m, jnp.arange(LANES) < N_META)
    return a_c, a_s, m_c, m_s


def _two_source(n0, nj):
    first = lambda col: (lambda i, j: (jnp.minimum(i, n0 - 1),
                                       col(jnp.where(i < n0, j, nj - 1))))
    second = lambda col: (lambda i, j: (jnp.maximum(i - n0, 0), col(jnp.where(i < n0, 0, j))))
    return first, second


def _merge_kernel(a_ref, f_ref, wa_ref, wf_ref, g0a_ref, g1a_ref, g0b_ref, g1b_ref, o_ref, *,
                  n0):
    ab = jnp.dot(a_ref[...], wa_ref[...], preferred_element_type=F32)
    fb = jnp.dot(f_ref[...], wf_ref[...], preferred_element_type=F32)
    i = pl.program_id(0)

    @pl.when(i < n0)
    def _():
        m = g0a_ref[...].astype(F32) * ab + g1a_ref[...].astype(F32) * fb
        o_ref[...] = m.astype(o_ref.dtype)

    @pl.when(i >= n0)
    def _():
        m = g0b_ref[...].astype(F32) * ab + g1b_ref[...].astype(F32) * fb
        o_ref[...] = m.astype(o_ref.dtype)


def _merge(attn, fmix, w_ab, w_fb, projs, *, gate_off, tm, tn):
    t, vw = attn.shape
    fw = fmix.shape[1]
    d = w_ab.shape[1]
    nj = d // tn
    n0 = projs[0].shape[0] // tm
    g0b = gate_off // tn
    g1b = (gate_off + d) // tn
    first, second = _two_source(n0, nj)
    return pl.pallas_call(
        functools.partial(_merge_kernel, n0=n0),
        out_shape=jax.ShapeDtypeStruct((t, d), BF16),
        grid=(t // tm, nj),
        in_specs=[pl.BlockSpec((tm, vw), lambda i, j: (i, 0)),
                  pl.BlockSpec((tm, fw), lambda i, j: (i, 0)),
                  pl.BlockSpec((vw, tn), lambda i, j: (0, j)),
                  pl.BlockSpec((fw, tn), lambda i, j: (0, j)),
                  pl.BlockSpec((tm, tn), first(lambda j: g0b + j)),
                  pl.BlockSpec((tm, tn), first(lambda j: g1b + j)),
                  pl.BlockSpec((tm, tn), second(lambda j: g0b + j)),
                  pl.BlockSpec((tm, tn), second(lambda j: g1b + j))],
        out_specs=pl.BlockSpec((tm, tn), lambda i, j: (i, j)),
        compiler_params=_cparams(("parallel", "arbitrary")),
        name="merge",
    )(attn, fmix, w_ab, w_fb, projs[0], projs[0], projs[1], projs[1])


def _outproj_kernel(m_ref, w_ref, ha_ref, hb_ref, o_ref, *, n0):
    acc = jnp.dot(m_ref[...], w_ref[...], preferred_element_type=F32)
    i = pl.program_id(0)

    @pl.when(i < n0)
    def _():
        o_ref[...] = ha_ref[...] + acc

    @pl.when(i >= n0)
    def _():
        o_ref[...] = hb_ref[...] + acc


def _outproj(merged, w_out, hs, *, tm, tn):
    t, d = merged.shape
    nj = d // tn
    n0 = hs[0].shape[0] // tm
    first, second = _two_source(n0, nj)
    return pl.pallas_call(
        functools.partial(_outproj_kernel, n0=n0),
        out_shape=jax.ShapeDtypeStruct((t, d), F32),
        grid=(t // tm, nj),
        in_specs=[pl.BlockSpec((tm, d), lambda i, j: (i, 0)),
                  pl.BlockSpec((d, tn), lambda i, j: (0, j)),
                  pl.BlockSpec((tm, tn), first(lambda j: j)),
                  pl.BlockSpec((tm, tn), second(lambda j: j))],
        out_specs=pl.BlockSpec((tm, tn), lambda i, j: (i, j)),
        compiler_params=_cparams(("parallel", "arbitrary")),
        name="outproj",
    )(merged, w_out, hs[0], hs[1])


def _pack_bf16_pairs(x):
    n = x.shape[1] // 2
    bits = lax.bitcast_convert_type(x.astype(BF16).astype(F32), jnp.uint32)
    return (bits[:, :n] >> 16) | (bits[:, n:] & jnp.uint32(0xFFFF0000))


def _unpack_bf16_pairs(w):
    lo = lax.bitcast_convert_type(w << 16, F32)
    hi = lax.bitcast_convert_type(w & jnp.uint32(0xFFFF0000), F32)
    return lo.astype(BF16), hi.astype(BF16)


def _router_kernel(h_ref, nw_ref, wh_ref, wl_ref, br_ref, o_ref, cnt_ref, xp_ref, carry_ref, *,
                   top_k):
    i = pl.program_id(0)

    @pl.when(i == 0)
    def _():
        carry_ref[...] = jnp.zeros_like(carry_ref)

    x = h_ref[...]
    tr = x.shape[0]
    ms = jnp.mean(x * x, axis=-1, keepdims=True)
    xn = x * lax.rsqrt(ms + EPS) * nw_ref[...]
    xp_ref[...] = _pack_bf16_pairs(xn)
    xh = xn.astype(BF16)
    xl = (xn - xh.astype(F32)).astype(BF16)
    logits = (jnp.dot(xh, wh_ref[...], preferred_element_type=F32)
              + jnp.dot(xh, wl_ref[...], preferred_element_type=F32)
              + jnp.dot(xl, wh_ref[...], preferred_element_type=F32)) + br_ref[...]
    lane = lax.broadcasted_iota(jnp.int32, logits.shape, 1)
    work = logits
    idxs, vals = [], []
    sel = jnp.zeros(logits.shape, F32)
    for _ in range(top_k):
        mx = jnp.max(work, axis=-1, keepdims=True)
        ix = jnp.min(jnp.where(work == mx, lane, LANES), axis=-1, keepdims=True)
        hit = lane == ix
        sel = jnp.where(hit, 1.0, sel)
        work = jnp.where(hit, -jnp.inf, work)
        idxs.append(ix)
        vals.append(mx)
    ex = [jnp.exp(v - vals[0]) for v in vals]
    den = ex[0]
    for e in ex[1:]:
        den = den + e
    inv = 1.0 / den
    r_i = lax.broadcasted_iota(jnp.int32, (tr, tr), 0)
    c_i = lax.broadcasted_iota(jnp.int32, (tr, tr), 1)
    tril = jnp.where(c_i < r_i, 1.0, 0.0).astype(BF16)
    rank_dense = jnp.dot(tril, sel.astype(BF16), preferred_element_type=F32) + carry_ref[...]
    out = jnp.zeros(logits.shape, F32)
    for k in range(top_k):
        rk = jnp.sum(jnp.where(lane == idxs[k], rank_dense, 0.0), axis=-1, keepdims=True)
        out = jnp.where(lane == k, idxs[k].astype(F32), out)
        out = jnp.where(lane == top_k + k, ex[k] * inv, out)
        out = jnp.where(lane == 2 * top_k + k, rk, out)
    o_ref[...] = out
    carry_ref[...] += jnp.sum(sel, axis=0, keepdims=True)
    cnt_ref[...] = carry_ref[...]


def _router(h1, norm_w, w_router, b_router, *, tr):
    t, d = h1.shape
    e = w_router.shape[1]
    wr = jnp.zeros((d, LANES), F32).at[:, :e].set(w_router)
    wr_hi = wr.astype(BF16)
    wr_lo = (wr - wr_hi.astype(F32)).astype(BF16)
    br = jnp.full((1, LANES), NEG_BIG, F32).at[0, :e].set(b_router)
    return pl.pallas_call(
        functools.partial(_router_kernel, top_k=TOP_K),
        out_shape=(jax.ShapeDtypeStruct((t, LANES), F32),
                   jax.ShapeDtypeStruct((1, LANES), F32),
                   jax.ShapeDtypeStruct((t, d // 2), jnp.uint32)),
        grid=(t // tr,),
        in_specs=[pl.BlockSpec((tr, d), lambda i: (i, 0)),
                  pl.BlockSpec((1, d), lambda i: (0, 0)),
                  pl.BlockSpec((d, LANES), lambda i: (0, 0)),
                  pl.BlockSpec((d, LANES), lambda i: (0, 0)),
                  pl.BlockSpec((1, LANES), lambda i: (0, 0))],
        out_specs=(pl.BlockSpec((tr, LANES), lambda i: (i, 0)),
                   pl.BlockSpec((1, LANES), lambda i: (0, 0)),
                   pl.BlockSpec((tr, d // 2), lambda i: (i, 0))),
        scratch_shapes=[pltpu.VMEM((1, LANES), F32)],
        compiler_params=_cparams(("arbitrary",)),
        name="router",
    )(h1, norm_w.reshape(1, d), wr_hi, wr_lo, br)


def _gather_start(idx_ref, idx0, src_hbm, dst_ref, sem, n):
    def issue(r, c):
        pltpu.make_async_copy(src_hbm.at[pl.ds(idx_ref[idx0 + r], 1)], dst_ref.at[pl.ds(r, 1)],
                              sem).start()
        return c

    lax.fori_loop(0, n, issue, 0, unroll=8)


def _gather_wait(src_hbm, dst_ref, sem, n):
    pltpu.make_async_copy(src_hbm.at[pl.ds(0, n)], dst_ref.at[pl.ds(0, n)], sem).wait()


def _dispatch_kernel(ps_ref, pn_ref, used_ref, slot_hbm, xp_ref, xs_hbm, slot_smem, zero_ref,
                     sem_i, sem_g, sem_z, *, top_k, tq, n_exp, n_tiles):
    i = pl.program_id(0)
    n = pl.num_programs(0)
    cp = pltpu.make_async_copy(slot_hbm.at[i], slot_smem, sem_i)
    cp.start()
    cp.wait()

    def issue(r, c):
        src = xp_ref.at[pl.ds(r, 1)]
        for k in range(top_k):
            pltpu.make_async_copy(src, xs_hbm.at[pl.ds(slot_smem[0, k * tq + r], 1)],
                                  sem_g).start()
        return c

    lax.fori_loop(0, tq, issue, 0, unroll=2)
    for k in range(top_k):
        pltpu.make_async_copy(xp_ref, xs_hbm.at[pl.ds(0, tq)], sem_g).wait()

    @pl.when(i == n - 1)
    def _():
        zero_ref[...] = jnp.zeros_like(zero_ref)
        zrow = zero_ref.at[pl.ds(0, 1)]

        def per_expert(e, c):
            start = ps_ref[e]
            cnt = pn_ref[e]

            def fill(r, c2):
                pltpu.make_async_copy(zrow, xs_hbm.at[pl.ds(start + r, 1)], sem_z).start()
                return c2

            def done(r, c2):
                pltpu.make_async_copy(zrow, xs_hbm.at[pl.ds(0, 1)], sem_z).wait()
                return c2

            lax.fori_loop(0, cnt, fill, 0)
            lax.fori_loop(0, cnt, done, 0)
            return c

        lax.fori_loop(0, n_exp, per_expert, 0)
        tm = zero_ref.shape[0]

        def fill_tile(j, c):
            cp_z = pltpu.make_async_copy(zero_ref, xs_hbm.at[pl.ds(j * tm, tm)], sem_z)
            cp_z.start()
            cp_z.wait()
            return c

        lax.fori_loop(used_ref[0], n_tiles, fill_tile, 0)


def _dispatch(slot_tiles, xp, pad_start, pad_n, tiles_used, *, n_tiles, tm, tq):
    t, half = xp.shape
    grid_spec = pltpu.PrefetchScalarGridSpec(
        num_scalar_prefetch=3,
        grid=(t // tq,),
        in_specs=[pl.BlockSpec(memory_space=pl.ANY),
                  pl.BlockSpec((tq, half), lambda i, ps, pn, used: (i, 0))],
        out_specs=pl.BlockSpec(memory_space=pl.ANY),
        scratch_shapes=[pltpu.SMEM((1, TOP_K * tq), jnp.int32),
                        pltpu.VMEM((tm, half), jnp.uint32),
                        pltpu.SemaphoreType.DMA,
                        pltpu.SemaphoreType.DMA,
                        pltpu.SemaphoreType.DMA],
    )
    return pl.pallas_call(
        functools.partial(_dispatch_kernel, top_k=TOP_K, tq=tq, n_exp=pad_start.shape[0],
                          n_tiles=n_tiles),
        out_shape=jax.ShapeDtypeStruct((n_tiles * tm, half), jnp.uint32),
        grid_spec=grid_spec,
        compiler_params=_cparams(("arbitrary",)),
        name="dispatch",
    )(pad_start, pad_n, tiles_used, slot_tiles, xp)


def _moe_kernel(te_ref, ns_ref, xs_hbm, wg_ref, wu_ref, bg_ref, bu_ref, wd_ref, bd_ref,
                o_ref, xbuf, xn_ref, act_ref, sem_x, *, nf, sub, nsub_max):
    del te_ref
    i = pl.program_id(0)
    s = pl.program_id(1)
    nsub = ns_ref[i]
    half = xbuf.shape[2]
    tf = act_ref.shape[2]
    full = nsub == nsub_max
    part = (nsub > 0) & (nsub < nsub_max)

    def rows(c):
        return pl.ds(pl.multiple_of(c * sub, sub), sub)

    @pl.when((s == 0) & (nsub > 0))
    def _():
        def load(c, b):
            r0 = pl.multiple_of((i * nsub_max + c) * sub, sub)
            return pltpu.make_async_copy(xs_hbm.at[pl.ds(r0, sub)], xbuf.at[b], sem_x.at[b])

        load(0, 0).start()

        def body(c, carry):
            b = c % 2

            @pl.when(c + 1 < nsub)
            def _():
                load(c + 1, 1 - b).start()

            load(c, b).wait()
            lo, hi = _unpack_bf16_pairs(xbuf[b])
            xn_ref[rows(c), 0:half] = lo
            xn_ref[rows(c), half:2 * half] = hi
            return carry

        lax.fori_loop(0, nsub, body, 0)

    def swiglu(xn):
        hg = jnp.dot(xn, wg_ref[...], preferred_element_type=F32) + bg_ref[...]
        hu = jnp.dot(xn, wu_ref[...], preferred_element_type=F32) + bu_ref[...]
        g = jnp.minimum(hg, SWIGLU_LIMIT)
        u = jnp.clip(hu, -SWIGLU_LIMIT, SWIGLU_LIMIT)
        return ((u + 1.0) * (g * jax.nn.sigmoid(SWIGLU_ALPHA * g))).astype(BF16)

    def down(act_of):
        acc = jnp.dot(act_of(0), wd_ref[0:tf, :], preferred_element_type=F32)
        for f in range(1, nf):
            acc += jnp.dot(act_of(f), wd_ref[f * tf:(f + 1) * tf, :],
                           preferred_element_type=F32)
        return acc + bd_ref[...]

    sf = jnp.minimum(s, nf - 1)

    @pl.when((s < nf) & full)
    def _():
        act_ref[sf] = swiglu(xn_ref[...])

    @pl.when((s < nf) & part)
    def _():
        def body(c, carry):
            act_ref[sf, rows(c), :] = swiglu(xn_ref[rows(c), :])
            return carry

        lax.fori_loop(0, nsub, body, 0)

    @pl.when((s >= nf) & full)
    def _():
        o_ref[...] = down(lambda f: act_ref[f])

    @pl.when((s >= nf) & jnp.logical_not(full))
    def _():
        def body(c, carry):
            o_ref[rows(c), :] = down(lambda f: act_ref[f, rows(c), :])
            return carry

        lax.fori_loop(0, nsub, body, 0)

        def zero(c, carry):
            o_ref[rows(c), :] = jnp.zeros((sub, o_ref.shape[1]), o_ref.dtype)
            return carry

        lax.fori_loop(nsub, nsub_max, zero, 0)


def _moe(xs, tile_expert, tile_nsub, w_gu, b_gu, w_dn, b_dn, *, sub, nsub_max, tf, tn):
    half = xs.shape[1]
    d = 2 * half
    n_exp, _, f2 = w_gu.shape
    fdim = f2 // 2
    nf = fdim // tf
    nn = d // tn
    tm = sub * nsub_max
    n_tiles = xs.shape[0] // tm
    last_n = nn - 1

    def ge(i, s, te, ns):
        nxt = jnp.minimum(i + 1, n_tiles - 1)
        return jnp.where((ns[i] != 0) & (s >= nf), te[nxt], te[i])

    def fi(i, s, te, ns):
        return jnp.where((ns[i] != 0) & (s < nf), s, 0)

    def ni(i, s, te, ns):
        return jnp.where(ns[i] != 0, jnp.maximum(s - nf, 0), last_n)

    def oi(i, s, te, ns):
        return jnp.maximum(s - nf, 0)

    b_gu3 = b_gu.reshape(n_exp, 1, f2)
    b_dn3 = b_dn.reshape(n_exp, 1, d)
    grid_spec = pltpu.PrefetchScalarGridSpec(
        num_scalar_prefetch=2,
        grid=(n_tiles, nf + nn),
        in_specs=[pl.BlockSpec(memory_space=pl.ANY),
                  pl.BlockSpec((None, d, tf), lambda *a: (ge(*a), 0, fi(*a))),
                  pl.BlockSpec((None, d, tf), lambda *a: (ge(*a), 0, nf + fi(*a))),
                  pl.BlockSpec((None, 1, tf), lambda *a: (ge(*a), 0, fi(*a))),
                  pl.BlockSpec((None, 1, tf), lambda *a: (ge(*a), 0, nf + fi(*a))),
                  pl.BlockSpec((None, fdim, tn),
                               lambda i, s, te, ns: (te[i], 0, ni(i, s, te, ns))),
                  pl.BlockSpec((None, 1, tn), lambda i, s, te, ns: (te[i], 0, ni(i, s, te, ns)))],
        out_specs=pl.BlockSpec((tm, tn), lambda i, s, te, ns: (i, oi(i, s, te, ns))),
        scratch_shapes=[pltpu.VMEM((2, sub, half), jnp.uint32),
                        pltpu.VMEM((tm, d), BF16),
                        pltpu.VMEM((nf, tm, tf), BF16),
                        pltpu.SemaphoreType.DMA((2,))],
    )
    return pl.pallas_call(
        functools.partial(_moe_kernel, nf=nf, sub=sub, nsub_max=nsub_max),
        out_shape=jax.ShapeDtypeStruct((n_tiles * tm, d), F32),
        grid_spec=grid_spec,
        compiler_params=_cparams(("arbitrary", "arbitrary"), vmem=MOE_VMEM_LIMIT),
        name="moe",
    )(tile_expert, tile_nsub, xs, w_gu, w_gu, b_gu3, b_gu3, w_dn, b_dn3)


def _combine_kernel(slot_hbm, y_hbm, h_ref, g_ref, nw_ref, o_ref, slot_smem, buf, sem_i, sem_g,
                    *, top_k, tile0):
    i = pl.program_id(0)
    n = pl.num_programs(0)
    tq = h_ref.shape[0]
    nrow = top_k * tq

    def fetch(tile, b):
        cp = pltpu.make_async_copy(slot_hbm.at[tile], slot_smem.at[b], sem_i)
        cp.start()
        cp.wait()
        _gather_start(slot_smem.at[b].at[0], 0, y_hbm, buf.at[b], sem_g.at[b], nrow)

    def consume(b):
        _gather_wait(y_hbm, buf.at[b], sem_g.at[b], nrow)
        g = g_ref[...]
        h = h_ref[...]
        for k in range(top_k):
            h = h + g[:, k:k + 1] * buf[b, k * tq:(k + 1) * tq, :]
        ms = jnp.mean(h * h, axis=-1, keepdims=True)
        o_ref[...] = h * lax.rsqrt(ms + EPS) * nw_ref[...]

    @pl.when(i == 0)
    def _():
        fetch(tile0, 0)

    for b in range(2):
        @pl.when(i % 2 == b)
        def _():
            @pl.when(i + 1 < n)
            def _():
                fetch(tile0 + i + 1, 1 - b)

            consume(b)


def _combine(slot_tiles, y_sorted, h1, gates, final_w, *, row0, rows, tq, name):
    d = h1.shape[1]
    kern = functools.partial(_combine_kernel, top_k=TOP_K, tile0=row0 // tq)
    return pl.pallas_call(
        kern,
        out_shape=jax.ShapeDtypeStruct((rows, d), F32),
        grid=(rows // tq,),
        in_specs=[pl.BlockSpec(memory_space=pl.ANY),
                  pl.BlockSpec(memory_space=pl.ANY),
                  pl.BlockSpec((tq, d), lambda i: (row0 // tq + i, 0)),
                  pl.BlockSpec((tq, TOP_K), lambda i: (row0 // tq + i, 0)),
                  pl.BlockSpec((1, d), lambda i: (0, 0))],
        out_specs=pl.BlockSpec((tq, d), lambda i: (i, 0)),
        scratch_shapes=[pltpu.SMEM((2, 1, TOP_K * tq), jnp.int32),
                        pltpu.VMEM((2, TOP_K * tq, d), F32),
                        pltpu.SemaphoreType.DMA,
                        pltpu.SemaphoreType.DMA((2,))],
        compiler_params=_cparams(("arbitrary",)),
        name=name,
    )(slot_tiles, y_sorted, h1, gates, final_w.reshape(1, d))


def _rope_rows(pos):
    inv_freq = 1.0 / (ROPE_THETA ** (jnp.arange(0, HEAD_DIM, 2, dtype=F32) / HEAD_DIM))
    ang = pos.astype(F32)[:, None] * inv_freq[None, :]
    cos = jnp.cos(ang)
    sin = jnp.sin(ang)
    return (jnp.concatenate([cos, cos], axis=-1), jnp.concatenate([-sin, sin], axis=-1))


def _channel_dft(fw):
    gc = fw // N_FOURIER_GROUPS
    idx = jnp.arange(gc, dtype=jnp.int32)
    ang = ((idx[:, None] * idx[None, :]) % gc).astype(F32) * (2.0 * math.pi / gc)
    nrm = 1.0 / math.sqrt(gc)
    eye = jnp.eye(N_FOURIER_GROUPS, dtype=F32)
    return jnp.concatenate([jnp.kron(eye, jnp.cos(ang) * nrm),
                            jnp.kron(eye, -jnp.sin(ang) * nrm)], axis=1)


def _lambda_init(layer):
    return 0.8 - 0.6 * math.exp(-0.3 * layer)


def _mix_block(xs, meta_h, layer, p):
    (norm_mix_w, w_in, lq1, lk1, lq2, lk2, subln_w, w_ab, w_fb, w_out) = p
    d = w_in.shape[0]
    vw, fw = w_ab.shape[0], w_fb.shape[0]
    qkw = (w_in.shape[1] - vw - fw - 2 * d) // 2
    n_heads = vw // (2 * HEAD_DIM)
    u_off = 2 * qkw + vw
    gate_off = u_off + 2 * fw
    lam_init = _lambda_init(layer)
    rows = [x.shape[0] * x.shape[1] for x in xs]
    t = sum(rows)
    tm = _pick(math.gcd(*rows), 1024)

    w_u = w_in[:, u_off:u_off + fw]
    w_ucs = _matmul(w_u, _channel_dft(fw), F32, tm=_pick(d, 512), tn=_pick(2 * fw, 512),
                    precision=lax.Precision.HIGHEST, name="fold")
    tn_in = _pick(math.gcd(qkw, math.gcd(vw, math.gcd(2 * fw, d))), 512)
    w_ext = jnp.concatenate([w_in[:, :u_off].astype(BF16), w_ucs.astype(BF16),
                             w_in[:, u_off + fw:].astype(BF16)], axis=1)

    mt = meta_h.shape[0]
    cos_m, sin_m = _rope_rows(jnp.arange(mt, dtype=jnp.int32))
    proj_meta = _inproj(meta_h, norm_mix_w, w_ext, cos_m, sin_m, qkw=qkw, gate_off=gate_off,
                        tm=mt, tn=tn_in, name="inproj_meta")

    lam_vecs = [v.reshape(1, HEAD_DIM).astype(F32) for v in (lq1, lk1, lq2, lk2)]
    attn = jnp.zeros((t, vw), BF16)
    fmix = jnp.zeros((t, fw), BF16)
    projs, hs = [], []
    row0 = 0
    for gi, x in enumerate(xs):
        nb, s, _ = x.shape
        h = x.reshape(nb * s, d)
        cos, sin_s = _rope_rows(N_META + (jnp.arange(nb * s, dtype=jnp.int32) % s))
        proj = _inproj(h, norm_mix_w, w_ext, cos, sin_s, qkw=qkw, gate_off=gate_off, tm=tm,
                       tn=tn_in, name=f"inproj{gi}")
        attn = _attention(proj, proj_meta, lam_vecs, subln_w, attn, row0=row0, nb=nb, s=s,
                          n_heads=n_heads, qkw=qkw, lam_init=lam_init, tq=_pick(s, 1024),
                          kc=_pick(s, 1024), name=f"attn{gi}")
        fmix = _fourier(proj, proj_meta, _dft_tables(s), fmix, row0=row0, nb=nb, s=s, fw=fw,
                        uc_off=u_off, tm=_pick(s, 512), tn=_pick(fw, 512), name=f"fourier{gi}")
        projs.append(proj)
        hs.append(h)
        row0 += nb * s

    tn = _pick(math.gcd(d, gate_off), 512)
    merged = _merge(attn, fmix, w_ab.astype(BF16), w_fb.astype(BF16), projs, gate_off=gate_off,
                    tm=tm, tn=tn)
    return _outproj(merged, w_out.astype(BF16), hs, tm=tm, tn=tn)


def _moe_block(h1, rows, p, final_norm_w):
    t, d = h1.shape
    norm_moe_w, w_router, b_router, w_gu, b_gu, w_dn, b_dn = p
    n_exp = w_router.shape[1]
    fdim = w_dn.shape[1]
    sub = _pick(t, MOE_SUB)
    tm = sub * MOE_NSUB

    rout, cnt, xp = _router(h1, norm_moe_w, w_router, b_router, tr=_pick(t, 256))
    e_idx = rout[:, 0:TOP_K].astype(jnp.int32)
    gates = rout[:, TOP_K:2 * TOP_K]
    rank = rout[:, 2 * TOP_K:3 * TOP_K].astype(jnp.int32)
    counts = cnt[0, :n_exp].astype(jnp.int32)

    nsub_e = (counts + sub - 1) // sub
    tiles_e = (nsub_e + MOE_NSUB - 1) // MOE_NSUB
    tile_end = jnp.cumsum(tiles_e)
    tile_base = tile_end - tiles_e
    slot = tile_base[e_idx] * tm + rank
    n_tiles = -(-t * TOP_K // tm) + n_exp
    a_pad = n_tiles * tm
    pad_start = (tile_base * tm + counts).astype(jnp.int32)
    pad_n = (tiles_e * tm - counts).astype(jnp.int32)
    tile_id = jnp.arange(n_tiles, dtype=jnp.int32)
    tile_expert = jnp.minimum(jnp.searchsorted(tile_end, tile_id, side='right'),
                              n_exp - 1).astype(jnp.int32)
    tile_nsub = jnp.clip(nsub_e[tile_expert] - (tile_id - tile_base[tile_expert]) * MOE_NSUB,
                         0, MOE_NSUB)
    tile_nsub = jnp.where(tile_id < tile_end[-1], tile_nsub, 0).astype(jnp.int32)
    last_e = tile_expert[jnp.maximum(tile_end[-1] - 1, 0)]
    tile_expert = jnp.where(tile_nsub != 0, tile_expert, last_e)

    tq = _pick(math.gcd(*rows), 128)
    slot_tiles = slot.reshape(t // tq, tq, TOP_K).transpose(0, 2, 1).reshape(t // tq, 1,
                                                                              TOP_K * tq)
    tq_d = _pick(t, 1024)
    slot_tiles_d = slot.reshape(t // tq_d, tq_d, TOP_K).transpose(0, 2, 1).reshape(
        t // tq_d, 1, TOP_K * tq_d)
    xs = _dispatch(slot_tiles_d, xp, pad_start, pad_n, tile_end[-1:].astype(jnp.int32),
                   n_tiles=n_tiles, tm=tm, tq=tq_d)
    y_sorted = _moe(xs, tile_expert, tile_nsub, w_gu.astype(BF16), b_gu, w_dn.astype(BF16), b_dn,
                    sub=sub, nsub_max=MOE_NSUB, tf=_pick(fdim, 512), tn=_pick(d, 512))
    outs = []
    row0 = 0
    for gi, r in enumerate(rows):
        outs.append(_combine(slot_tiles, y_sorted, h1, gates, final_norm_w, row0=row0, rows=r,
                             tq=tq, name=f"combine{gi}"))
        row0 += r
    return outs


def kernel(x_prompt, x_sample, meta_tokens, norm_mix_w, w_in, lambda_q1, lambda_k1, lambda_q2,
           lambda_k2, attn_subln_w, w_attn_branch, w_fourier_branch, w_out, norm_moe_w,
           w_router, b_router, w_gate_up, b_gate_up, w_down, b_down, final_norm_w):
    depth = norm_mix_w.shape[0]
    assert depth == 1, "meta rows are dropped after the first layer's mixing stage"
    d = x_prompt.shape[-1]
    xs = (x_prompt, x_sample)
    meta_h = jnp.zeros((LANES, d), F32).at[:N_META].set(meta_tokens.astype(F32))

    mix_p = (norm_mix_w[0], w_in[0], lambda_q1[0], lambda_k1[0], lambda_q2[0], lambda_k2[0],
             attn_subln_w[0], w_attn_branch[0], w_fourier_branch[0], w_out[0])
    h1 = _mix_block(xs, meta_h, 0, mix_p)
    moe_p = (norm_moe_w[0], w_router[0], b_router[0], w_gate_up[0], b_gate_up[0], w_down[0],
             b_down[0])
    rows = [x.shape[0] * x.shape[1] for x in xs]
    outs = _moe_block(h1, rows, moe_p, final_norm_w)
    return tuple(o.reshape(x.shape) for o, x in zip(outs, xs))
```
